```python
import math
import jax, jax.numpy as jnp
from jax import lax
import numpy as np

D_MODEL = 4096
BATCH = 4
SEQ = 4096
DEPTH = 2

CHUNK = 64
MIX_W = D_MODEL
W_SSM = MIX_W // 2
SSM_P = 16
SSM_G = W_SSM // SSM_P
SSM_N = 64
FOX_DH = 128
W_FOX = MIX_W - W_SSM
FOX_H = W_FOX // FOX_DH
IN_W = W_SSM + 3 * W_FOX + FOX_H
Q_BLOCK = 128
D_FF = ((8 * D_MODEL // 3 + 255) // 256) * 256
MEM_LEN = 256
X_HEADS = 4
X_DH = D_MODEL // X_HEADS
X_W = X_HEADS * X_DH
EPS = 1e-6

kernel_name = "hybrid_s5_fox_macaron_memory_block"


def rms_norm(x, g):
    xf = x.astype(jnp.float32)
    y = xf * lax.rsqrt(jnp.mean(xf * xf, axis=-1, keepdims=True) + EPS)
    return (y * g.astype(jnp.float32)).astype(x.dtype)


def swiglu(x, w_gate, w_up, w_down):
    return (jax.nn.silu(x @ w_gate) * (x @ w_up)) @ w_down


def _complex_linear_recurrence(e1, e2):
    a1r, a1i, b1r, b1i = e1
    a2r, a2i, b2r, b2i = e2
    return (a2r * a1r - a2i * a1i,
            a2r * a1i + a2i * a1r,
            a2r * b1r - a2i * b1i + b2r,
            a2r * b1i + a2i * b1r + b2i)


def s5_mixer(u, A_re, A_im, log_step, B_re, B_im, C_re, C_im, D, w_glu):
    f32 = jnp.float32
    bsz, L, _ = u.shape
    uf = u.astype(f32).reshape(bsz, L, SSM_G, SSM_P)
    ar, ai = A_re.astype(f32), A_im.astype(f32)
    dt = jnp.exp(log_step.astype(f32))[:, None]
    mag = jnp.exp(ar * dt)
    abar_r, abar_i = mag * jnp.cos(ai * dt), mag * jnp.sin(ai * dt)
    den = ar * ar + ai * ai
    pr, pi_ = abar_r - 1.0, abar_i
    coef_r = (pr * ar + pi_ * ai) / den
    coef_i = (pi_ * ar - pr * ai) / den
    Br, Bi = B_re.astype(f32), B_im.astype(f32)
    bbar_r = coef_r[..., None] * Br - coef_i[..., None] * Bi
    bbar_i = coef_r[..., None] * Bi + coef_i[..., None] * Br
    br = jnp.einsum('blgp,gnp->blgn', uf, bbar_r)
    bi = jnp.einsum('blgp,gnp->blgn', uf, bbar_i)
    a_r = jnp.broadcast_to(abar_r[None, None], (1, L, SSM_G, SSM_N))
    a_i = jnp.broadcast_to(abar_i[None, None], (1, L, SSM_G, SSM_N))
    _, _, xr, xi = lax.associative_scan(_complex_linear_recurrence, (a_r, a_i, br, bi), axis=1)
    y = (jnp.einsum('blgn,gpn->blgp', xr, C_re.astype(f32))
         - jnp.einsum('blgn,gpn->blgp', xi, C_im.astype(f32))
         + D.astype(f32).reshape(SSM_G, SSM_P) * uf)
    y = jax.nn.gelu(y.reshape(bsz, L, W_SSM))
    y = y * jax.nn.sigmoid(y @ w_glu.astype(f32))
    return y.astype(u.dtype)


def fox_attention(q, k, v, f_logit, b_f, q_gain, k_gain):
    f32 = jnp.float32
    bsz, L, _ = q.shape
    q = rms_norm(q.reshape(bsz, L, FOX_H, FOX_DH), q_gain)
    k = rms_norm(k.reshape(bsz, L, FOX_H, FOX_DH), k_gain)
    v = v.reshape(bsz, L, FOX_H, FOX_DH)
    log_f = jax.nn.log_sigmoid(f_logit.astype(f32) + b_f.astype(f32))
    c = jnp.cumsum(log_f, axis=1).transpose(0, 2, 1)
    qh, kh, vh = (t.transpose(0, 2, 1, 3) for t in (q, k, v))
    nb = L // Q_BLOCK
    qb = qh.reshape(bsz, FOX_H, nb, Q_BLOCK, FOX_DH).transpose(2, 0, 1, 3, 4)
    cb = c.reshape(bsz, FOX_H, nb, Q_BLOCK).transpose(2, 0, 1, 3)
    kpos = jnp.arange(L)
    scale = FOX_DH ** -0.5

    def block(args):
        qblk, cblk, i = args
        qpos = i * Q_BLOCK + jnp.arange(Q_BLOCK)
        s = jnp.einsum('bhqd,bhkd->bhqk', qblk, kh, preferred_element_type=f32) * scale
        s = s + cblk[..., :, None] - c[:, :, None, :]
        s = jnp.where(kpos[None, :] <= qpos[:, None], s, -jnp.inf)
        p = jax.nn.softmax(s, axis=-1)
        return jnp.einsum('bhqk,bhkd->bhqd', p.astype(vh.dtype), vh)

    out = lax.map(block, (qb, cb, jnp.arange(nb)))
    return out.transpose(1, 0, 3, 2, 4).reshape(bsz, L, W_FOX)


def memory_cross_attention(hn, memn, wq, wk, wv, q_gain, k_gain, wo):
    f32 = jnp.float32
    bsz, L, _ = hn.shape
    M = memn.shape[1]
    q = rms_norm((hn @ wq).reshape(bsz, L, X_HEADS, X_DH), q_gain)
    k = rms_norm((memn @ wk).reshape(bsz, M, X_HEADS, X_DH), k_gain)
    v = (memn @ wv).reshape(bsz, M, X_HEADS, X_DH)
    s = jnp.einsum('blhd,bmhd->bhlm', q, k, preferred_element_type=f32) * (X_DH ** -0.5)
    p = jax.nn.softmax(s, axis=-1)
    o = jnp.einsum('bhlm,bmhd->blhd', p.astype(v.dtype), v).reshape(bsz, L, X_W)
    return o @ wo


def setup_inputs(seed: int = 0) -> dict:
    key = jax.random.key(seed)
    ks = iter(jax.random.split(key, 64))
    f32 = jnp.float32

    def nrm(shape, scale):
        return jax.random.normal(next(ks), shape, f32) * scale

    def gain(shape):
        return 1.0 + 0.02 * jax.random.normal(next(ks), shape, f32)

    L = DEPTH
    n_idx = jnp.arange(SSM_N, dtype=f32)
    inp = {}
    inp['x'] = jax.random.normal(next(ks), (BATCH, SEQ, D_MODEL), f32)
    inp['mem'] = jax.random.normal(next(ks), (BATCH, MEM_LEN, D_MODEL), f32)
    inp['ffn1_norm'] = gain((L, D_MODEL))
    inp['ffn1_w_gate'] = nrm((L, D_MODEL, D_FF), D_MODEL ** -0.5)
    inp['ffn1_w_up'] = nrm((L, D_MODEL, D_FF), D_MODEL ** -0.5)
    inp['ffn1_w_down'] = nrm((L, D_FF, D_MODEL), D_FF ** -0.5)
    inp['mix_norm'] = gain((L, D_MODEL))
    inp['w_in'] = nrm((L, D_MODEL, IN_W), D_MODEL ** -0.5)
    inp['ssm_A_re'] = -0.5 + 0.01 * jax.random.normal(next(ks), (L, SSM_G, SSM_N), f32)
    inp['ssm_A_im'] = math.pi * n_idx + 0.01 * jax.random.normal(next(ks), (L, SSM_G, SSM_N), f32)
    inp['ssm_log_step'] = jax.random.uniform(next(ks), (L, SSM_G), f32, math.log(1e-3), math.log(1e-1))
    inp['ssm_B_re'] = nrm((L, SSM_G, SSM_N, SSM_P), (2 * SSM_P) ** -0.5)
    inp['ssm_B_im'] = nrm((L, SSM_G, SSM_N, SSM_P), (2 * SSM_P) ** -0.5)
    inp['ssm_C_re'] = nrm((L, SSM_G, SSM_P, SSM_N), (2 * SSM_N) ** -0.5)
    inp['ssm_C_im'] = nrm((L, SSM_G, SSM_P, SSM_N), (2 * SSM_N) ** -0.5)
    inp['ssm_D'] = nrm((L, W_SSM), 1.0)
    inp['ssm_w_glu'] = nrm((L, W_SSM, W_SSM), W_SSM ** -0.5)
    inp['ssm_out_norm'] = gain((L, W_SSM))
    inp['fox_b_f'] = jax.random.uniform(next(ks), (L, FOX_H), f32, 1.0, 6.0)
    inp['fox_q_norm'] = gain((L, FOX_DH))
    inp['fox_k_norm'] = gain((L, FOX_DH))
    inp['fox_out_norm'] = gain((L, W_FOX))
    inp['w_out'] = nrm((L, MIX_W, D_MODEL), MIX_W ** -0.5)
    inp['xattn_norm'] = gain((L, D_MODEL))
    inp['mem_norm'] = gain((L, D_MODEL))
    inp['xattn_wq'] = nrm((L, D_MODEL, X_W), D_MODEL ** -0.5)
    inp['xattn_wk'] = nrm((L, D_MODEL, X_W), D_MODEL ** -0.5)
    inp['xattn_wv'] = nrm((L, D_MODEL, X_W), D_MODEL ** -0.5)
    inp['xattn_q_norm'] = gain((L, X_DH))
    inp['xattn_k_norm'] = gain((L, X_DH))
    inp['xattn_wo'] = nrm((L, X_W, D_MODEL), X_W ** -0.5)
    inp['ffn2_norm'] = gain((L, D_MODEL))
    inp['ffn2_w_gate'] = nrm((L, D_MODEL, D_FF), D_MODEL ** -0.5)
    inp['ffn2_w_up'] = nrm((L, D_MODEL, D_FF), D_MODEL ** -0.5)
    inp['ffn2_w_down'] = nrm((L, D_FF, D_MODEL), D_FF ** -0.5)
    inp['final_norm'] = gain((L, D_MODEL))
    return inp


def reference(x, mem, ffn1_norm, ffn1_w_gate, ffn1_w_up, ffn1_w_down, mix_norm, w_in,
              ssm_A_re, ssm_A_im, ssm_log_step, ssm_B_re, ssm_B_im, ssm_C_re, ssm_C_im,
              ssm_D, ssm_w_glu, ssm_out_norm, fox_b_f, fox_q_norm, fox_k_norm, fox_out_norm,
              w_out, xattn_norm, mem_norm, xattn_wq, xattn_wk, xattn_wv, xattn_q_norm,
              xattn_k_norm, xattn_wo, ffn2_norm, ffn2_w_gate, ffn2_w_up, ffn2_w_down, final_norm):
    h = x
    o_q = W_SSM
    o_k = o_q + W_FOX
    o_v = o_k + W_FOX
    o_f = o_v + W_FOX
    for l in range(DEPTH):
        h = h + 0.5 * swiglu(rms_norm(h, ffn1_norm[l]), ffn1_w_gate[l], ffn1_w_up[l], ffn1_w_down[l])
        z = rms_norm(h, mix_norm[l]) @ w_in[l]
        y_ssm = s5_mixer(z[..., :o_q], ssm_A_re[l], ssm_A_im[l], ssm_log_step[l],
                         ssm_B_re[l], ssm_B_im[l], ssm_C_re[l], ssm_C_im[l],
                         ssm_D[l], ssm_w_glu[l])
        y_fox = fox_attention(z[..., o_q:o_k], z[..., o_k:o_v], z[..., o_v:o_f], z[..., o_f:],
                              fox_b_f[l], fox_q_norm[l], fox_k_norm[l])
        y = jnp.concatenate([rms_norm(y_ssm, ssm_out_norm[l]),
                             rms_norm(y_fox, fox_out_norm[l])], axis=-1)
        h = h + y @ w_out[l]
        h = h + memory_cross_attention(rms_norm(h, xattn_norm[l]), rms_norm(mem, mem_norm[l]),
                                       xattn_wq[l], xattn_wk[l], xattn_wv[l],
                                       xattn_q_norm[l], xattn_k_norm[l], xattn_wo[l])
        h = h + 0.5 * swiglu(rms_norm(h, ffn2_norm[l]), ffn2_w_gate[l], ffn2_w_up[l], ffn2_w_down[l])
        h = rms_norm(h, final_norm[l])
    return h
```

```python
import functools
import math

import jax
import jax.numpy as jnp
from jax import lax
from jax.experimental import pallas as pl
from jax.experimental.pallas import tpu as pltpu

F32 = jnp.float32
BF16 = jnp.bfloat16
EPS = 1e-6
NEG_BIG = -1e30

V7X_VMEM_LIMIT_BYTES = 56 * 1024 * 1024
LANES = 128
SUBLANES = 8


def _params(*sem):
    return pltpu.CompilerParams(dimension_semantics=sem,
                                vmem_limit_bytes=V7X_VMEM_LIMIT_BYTES)


def _rms(x, g):
    ms = jnp.mean(x * x, axis=-1, keepdims=True)
    return x * lax.rsqrt(ms + EPS) * g


def _rmsnorm_kernel(x_ref, g_ref, o_ref):
    o_ref[...] = _rms(x_ref[...].astype(F32), g_ref[...]).astype(o_ref.dtype)


def rmsnorm(x, g, out_dtype, tm=256):
    m, k = x.shape
    tm = min(tm, m)
    return pl.pallas_call(
        _rmsnorm_kernel,
        grid=(m // tm,),
        in_specs=[pl.BlockSpec((tm, k), lambda i: (i, 0)),
                  pl.BlockSpec((1, k), lambda i: (0, 0))],
        out_specs=pl.BlockSpec((tm, k), lambda i: (i, 0)),
        out_shape=jax.ShapeDtypeStruct((m, k), out_dtype),
        compiler_params=_params("parallel"),
        name="rmsnorm",
    )(x, g.reshape(1, k).astype(F32))


def _mm_kernel(*refs, has_res, norm_group, norm_lo, norm_hi):
    x_ref, w_ref = refs[0], refs[1]
    pos = 2
    res_ref = g_ref = None
    if has_res:
        res_ref = refs[pos]
        pos += 1
    if norm_group:
        g_ref = refs[pos]
        pos += 1
    o_ref = refs[pos]

    acc = jnp.dot(x_ref[...], w_ref[...], preferred_element_type=F32)
    if has_res:
        acc = acc + res_ref[...]
    if not norm_group:
        o_ref[...] = acc.astype(o_ref.dtype)
        return

    j = pl.program_id(1)
    in_range = jnp.logical_and(j >= norm_lo, j < norm_hi)

    @pl.when(in_range)
    def _():
        for c in range(acc.shape[1] // norm_group):
            sl = slice(c * norm_group, (c + 1) * norm_group)
            o_ref[:, sl] = _rms(acc[:, sl], g_ref[:, sl]).astype(o_ref.dtype)

    @pl.when(jnp.logical_not(in_range))
    def _():
        o_ref[...] = acc.astype(o_ref.dtype)


def matmul(x, w, out_dtype, *, tm=512, tn=512, n_cols=None, res=None,
           norm_gain=None, norm_group=0, norm_cols=None):
    m, k = x.shape
    n = w.shape[1] if n_cols is None else n_cols
    tm, tn = min(tm, m), min(tn, n)
    assert m % tm == 0 and n % tn == 0
    in_specs = [pl.BlockSpec((tm, k), lambda i, j: (i, 0)),
                pl.BlockSpec((k, tn), lambda i, j: (0, j))]
    args = [x, w]
    if res is not None:
        in_specs.append(pl.BlockSpec((tm, tn), lambda i, j: (i, j)))
        args.append(res)
    norm_lo = norm_hi = 0
    if norm_group:
        lo, hi = (0, n) if norm_cols is None else norm_cols
        assert tn % norm_group == 0 and lo % tn == 0 and hi % tn == 0
        norm_lo, norm_hi = lo // tn, hi // tn
        in_specs.append(pl.BlockSpec((1, tn), lambda i, j: (0, j)))
        args.append(norm_gain.reshape(1, n).astype(F32))
    return pl.pallas_call(
        functools.partial(_mm_kernel, has_res=res is not None, norm_group=norm_group,
                          norm_lo=norm_lo, norm_hi=norm_hi),
        grid=(m // tm, n // tn),
        in_specs=in_specs,
        out_specs=pl.BlockSpec((tm, tn), lambda i, j: (i, j)),
        out_shape=jax.ShapeDtypeStruct((m, n), out_dtype),
        compiler_params=_params("parallel", "parallel"),
        name="matmul",
    )(*args)


def _ffn_kernel(*refs, final_norm):
    if final_norm:
        x_ref, g_ref, wg_ref, wu_ref, wd_ref, fg_ref, o_ref, n_scr = refs
    else:
        x_ref, g_ref, wg_ref, wu_ref, wd_ref, o_ref, n_scr = refs
        fg_ref = None
    j = pl.program_id(1)

    @pl.when(j == 0)
    def _():
        x = x_ref[...]
        n_scr[...] = _rms(x, g_ref[...]).astype(BF16)
        o_ref[...] = x

    n = n_scr[...]
    gate = jnp.dot(n, wg_ref[...], preferred_element_type=F32)
    up = jnp.dot(n, wu_ref[...], preferred_element_type=F32)
    hid = (gate * jax.nn.sigmoid(gate) * up).astype(BF16)
    o_ref[...] += jnp.dot(hid, wd_ref[...], preferred_element_type=F32)

    if final_norm:
        @pl.when(j == pl.num_programs(1) - 1)
        def _():
            o_ref[...] = _rms(o_ref[...], fg_ref[...])


def ffn(x, g, wg, wu, wd_half, final_gain=None, *, tm=512, tf=256):
    m, d = x.shape
    dff = wg.shape[1]
    tm, tf = min(tm, m), min(tf, dff)
    assert m % tm == 0 and dff % tf == 0
    in_specs = [pl.BlockSpec((tm, d), lambda i, j: (i, 0), pipeline_mode=pl.Buffered(1)),
                pl.BlockSpec((1, d), lambda i, j: (0, 0)),
                pl.BlockSpec((d, tf), lambda i, j: (0, j)),
                pl.BlockSpec((d, tf), lambda i, j: (0, j)),
                pl.BlockSpec((tf, d), lambda i, j: (j, 0))]
    args = [x, g.reshape(1, d).astype(F32), wg, wu, wd_half]
    if final_gain is not None:
        in_specs.append(pl.BlockSpec((1, d), lambda i, j: (0, 0)))
        args.append(final_gain.reshape(1, d).astype(F32))
    return pl.pallas_call(
        functools.partial(_ffn_kernel, final_norm=final_gain is not None),
        grid=(m // tm, dff // tf),
        in_specs=in_specs,
        out_specs=pl.BlockSpec((tm, d), lambda i, j: (i, 0)),
        out_shape=jax.ShapeDtypeStruct((m, d), F32),
        scratch_shapes=[pltpu.VMEM((tm, d), BF16)],
        compiler_params=_params("parallel", "arbitrary"),
        name="ffn",
    )(*args)


def _s5_disc_kernel(are_ref, aim_ref, ls_ref, bre_ref, bim_ref,
                    abr_ref, abi_ref, bbr_ref, bbi_ref):
    ar, ai = are_ref[...], aim_ref[...]
    dt = jnp.exp(ls_ref[...])
    mag = jnp.exp(ar * dt)
    abar_r = mag * jnp.cos(ai * dt)
    abar_i = mag * jnp.sin(ai * dt)
    den = ar * ar + ai * ai
    pr, pi_ = abar_r - 1.0, abar_i
    coef_r = (pr * ar + pi_ * ai) / den
    coef_i = (pi_ * ar - pr * ai) / den
    br, bi = bre_ref[...], bim_ref[...]
    abr_ref[...] = abar_r
    abi_ref[...] = abar_i
    bbr_ref[...] = coef_r * br - coef_i * bi
    bbi_ref[...] = coef_r * bi + coef_i * br


def s5_discretise(a_re, a_im, log_step, b_re, b_im):
    g, n = a_re.shape
    p = b_re.shape[-1]
    col = jax.ShapeDtypeStruct((g, n, 1), F32)
    mat = jax.ShapeDtypeStruct((g, n, p), F32)
    ls = jnp.broadcast_to(log_step.reshape(g, 1, 1), (g, n, 1))
    abr, abi, bbr, bbi = pl.pallas_call(
        _s5_disc_kernel,
        out_shape=(col, col, mat, mat),
        compiler_params=pltpu.CompilerParams(vmem_limit_bytes=V7X_VMEM_LIMIT_BYTES),
        name="s5_discretise",
    )(a_re.reshape(g, n, 1), a_im.reshape(g, n, 1), ls, b_re, b_im)
    return abr.reshape(g, n), abi.reshape(g, n), bbr, bbi


GROUPS_PER_HALF = 8


def _gelu_tanh(x):
    c = math.sqrt(2.0 / math.pi)
    return 0.5 * x * (1.0 + jnp.tanh(c * (x + 0.044715 * (x * x * x))))


def _s5_kernel(u_ref, rb_ref, rc_ref, ar_ref, ai_ref, d_ref, y_ref,
               lhs_scr, sr_scr, si_scr, o2_scr, xr_scr, xi_scr, *, nb, hs, tt, sw, hw):
    @pl.when(pl.program_id(1) == 0)
    def _():
        xr_scr[...] = jnp.zeros_like(xr_scr)
        xi_scr[...] = jnp.zeros_like(xi_scr)

    zeros = jnp.zeros((tt, hw), F32)
    for b in range(nb):
        ub = u_ref[b].astype(F32)
        for hh in range(hs):
            rows = pl.ds(hh * nb + b, tt, stride=SUBLANES)
            for c in range(hs):
                lhs_scr[c, rows, :] = ub[:, c * hw:(c + 1) * hw] if c == hh else zeros

    lhs = jnp.concatenate([lhs_scr[c] for c in range(hs)], axis=-1).astype(BF16)
    sr_scr[...] = jnp.dot(lhs, rb_ref[0, :, :sw], preferred_element_type=F32)
    si_scr[...] = jnp.dot(lhs, rb_ref[0, :, sw:], preferred_element_type=F32)

    ar, ai = ar_ref[0], ai_ref[0]

    def step(t, carry):
        xr, xi = carry
        rows = pl.ds(pl.multiple_of(t * SUBLANES, SUBLANES), SUBLANES)
        nxr = ar * xr - ai * xi + sr_scr[rows, :]
        nxi = ar * xi + ai * xr + si_scr[rows, :]
        sr_scr[rows, :] = nxr
        si_scr[rows, :] = nxi
        return nxr, nxi

    xr, xi = lax.fori_loop(0, tt, step, (xr_scr[...], xi_scr[...]), unroll=8)
    xr_scr[...] = xr
    xi_scr[...] = xi

    o2 = (jnp.dot(sr_scr[...].astype(BF16), rc_ref[0, :sw, :], preferred_element_type=F32)
          + jnp.dot(si_scr[...].astype(BF16), rc_ref[0, sw:, :], preferred_element_type=F32))
    for c in range(hs):
        o2_scr[c] = o2[:, c * hw:(c + 1) * hw]

    for b in range(nb):
        parts = [o2_scr[hh, pl.ds(hh * nb + b, tt, stride=SUBLANES), :] for hh in range(hs)]
        yb = jnp.concatenate(parts, axis=-1) + d_ref[...] * u_ref[b].astype(F32)
        y_ref[b] = _gelu_tanh(yb).astype(y_ref.dtype)


def s5_core(z3, abar_r, abar_i, bbar_r, bbar_i, c_re, c_im, d_skip, *, tt=256):
    nb, seq, _ = z3.shape
    g, n, p = bbar_r.shape
    assert SUBLANES % nb == 0
    hs = SUBLANES // nb
    gph = GROUPS_PER_HALF
    assert g % (hs * gph) == 0
    nblk = g // (hs * gph)
    hw, sw = gph * p, gph * n
    cw = hs * hw
    assert hw == LANES and sw % LANES == 0
    tt = min(tt, seq)
    assert seq % tt == 0

    eye = jnp.eye(gph, dtype=F32)

    def in_proj(bb):
        bb = bb.reshape(nblk, hs, gph, n, p)
        return jnp.einsum("jhqnp,qr->jhqprn", bb, eye).reshape(nblk, cw, sw)

    def out_proj(cc):
        cc = cc.reshape(nblk, hs, gph, p, n)
        return jnp.einsum("jhrpn,qr->jqnhrp", cc, eye).reshape(nblk, sw, cw)

    rb = jnp.concatenate([in_proj(bbar_r), in_proj(bbar_i)], axis=-1).astype(BF16)
    rc = jnp.concatenate([out_proj(c_re), out_proj(-c_im)], axis=1).astype(BF16)

    def rows(a):
        a = a.reshape(nblk, hs, 1, sw)
        return jnp.broadcast_to(a, (nblk, hs, nb, sw)).reshape(nblk, hs * nb, sw)

    return pl.pallas_call(
        functools.partial(_s5_kernel, nb=nb, hs=hs, tt=tt, sw=sw, hw=hw),
        grid=(nblk, seq // tt),
        in_specs=[pl.BlockSpec((nb, tt, cw), lambda j, t: (0, t, j)),
                  pl.BlockSpec((1, cw, 2 * sw), lambda j, t: (j, 0, 0)),
                  pl.BlockSpec((1, 2 * sw, cw), lambda j, t: (j, 0, 0)),
                  pl.BlockSpec((1, SUBLANES, sw), lambda j, t: (j, 0, 0)),
                  pl.BlockSpec((1, SUBLANES, sw), lambda j, t: (j, 0, 0)),
                  pl.BlockSpec((1, cw), lambda j, t: (0, j))],
        out_specs=pl.BlockSpec((nb, tt, cw), lambda j, t: (0, t, j)),
        out_shape=jax.ShapeDtypeStruct((nb, seq, g * p), BF16),
        scratch_shapes=[pltpu.VMEM((hs, SUBLANES * tt, hw), F32),
                        pltpu.VMEM((SUBLANES * tt, sw), F32),
                        pltpu.VMEM((SUBLANES * tt, sw), F32),
                        pltpu.VMEM((hs, SUBLANES * tt, hw), F32),
                        pltpu.VMEM((SUBLANES, sw), F32),
                        pltpu.VMEM((SUBLANES, sw), F32)],
        compiler_params=_params("parallel", "arbitrary"),
        name="s5_core",
    )(z3, rb, rc, rows(abar_r), rows(abar_i), d_skip.reshape(1, g * p).astype(F32))


def _glu_kernel(y_ref, w_ref, g_ref, o_ref):
    y = y_ref[...]
    v = jnp.dot(y, w_ref[...], preferred_element_type=F32)
    o = y.astype(F32) * jax.nn.sigmoid(v)
    o_ref[...] = _rms(o, g_ref[...]).astype(o_ref.dtype)


def glu_norm(y, w, g, *, tm=512):
    m, k = y.shape
    tm = min(tm, m)
    return pl.pallas_call(
        _glu_kernel,
        grid=(m // tm,),
        in_specs=[pl.BlockSpec((tm, k), lambda i: (i, 0)),
                  pl.BlockSpec((k, k), lambda i: (0, 0)),
                  pl.BlockSpec((1, k), lambda i: (0, 0))],
        out_specs=pl.BlockSpec((tm, k), lambda i: (i, 0)),
        out_shape=jax.ShapeDtypeStruct((m, k), BF16),
        compiler_params=_params("parallel"),
        name="glu_norm",
    )(y, w, g.reshape(1, k).astype(F32))


def _cumsum_kernel(f_ref, bf_ref, c_ref, *, chunk):
    seq = f_ref.shape[1]
    row = lax.broadcasted_iota(jnp.int32, (chunk, chunk), 0)
    col = lax.broadcasted_iota(jnp.int32, (chunk, chunk), 1)
    tri = (col <= row).astype(F32)
    carry = jnp.zeros((1, f_ref.shape[2]), F32)
    for ci in range(seq // chunk):
        sl = slice(ci * chunk, (ci + 1) * chunk)
        x = f_ref[0, sl, :] + bf_ref[...]
        log_f = jnp.minimum(x, 0.0) - jnp.log1p(jnp.exp(-jnp.abs(x)))
        cs = jnp.dot(tri, log_f, precision=lax.Precision.HIGHEST,
                     preferred_element_type=F32) + carry
        c_ref[0, sl, :] = cs
        carry = cs[chunk - 1:chunk, :]


def forget_cumsum(f3, b_f_row, *, chunk=256):
    nb, seq, w = f3.shape
    chunk = min(chunk, seq)
    return pl.pallas_call(
        functools.partial(_cumsum_kernel, chunk=chunk),
        grid=(nb,),
        in_specs=[pl.BlockSpec((1, seq, w), lambda b: (b, 0, 0)),
                  pl.BlockSpec((1, w), lambda b: (0, 0))],
        out_specs=pl.BlockSpec((1, seq, w), lambda b: (b, 0, 0)),
        out_shape=jax.ShapeDtypeStruct((nb, seq, w), F32),
        compiler_params=_params("parallel"),
        name="forget_cumsum",
    )(f3, b_f_row)


def _fox_kernel(q_ref, k_ref, v_ref, cq_ref, ck_ref, o_ref, *, tq):
    i = pl.program_id(2)
    q = q_ref[0]
    cq = cq_ref[0, 0]

    def block(j, carry, diagonal):
        m, l, acc = carry
        rows = pl.ds(pl.multiple_of(j * tq, tq), tq)
        k = k_ref[0, rows, :]
        v = v_ref[0, rows, :]
        s = lax.dot_general(q, k, (((1,), (1,)), ((), ())), preferred_element_type=F32)
        s = s + (cq - ck_ref[0, 0, j])
        if diagonal:
            r = lax.broadcasted_iota(jnp.int32, s.shape, 0)
            c = lax.broadcasted_iota(jnp.int32, s.shape, 1)
            s = jnp.where(c <= r, s, NEG_BIG)
        m_new = jnp.maximum(m, jnp.max(s, axis=-1, keepdims=True))
        alpha = jnp.exp(m - m_new)
        p = jnp.exp(s - m_new)
        l = alpha * l + jnp.sum(p, axis=-1, keepdims=True)
        acc = alpha * acc + jnp.dot(p.astype(v.dtype), v, preferred_element_type=F32)
        return m_new, l, acc

    dh = q.shape[-1]
    init = (jnp.full((tq, 1), NEG_BIG, F32), jnp.zeros((tq, 1), F32),
            jnp.zeros((tq, dh), F32))
    carry = lax.fori_loop(0, i, lambda j, c: block(j, c, False), init)
    _, l, acc = block(i, carry, True)
    o_ref[0] = (acc / l).astype(o_ref.dtype)


def fox_attention(z3, c, *, q_off, k_off, v_off, heads, dh, tq=512):
    nb, seq, _ = z3.shape
    tq = min(tq, seq)
    nq = seq // tq
    assert seq % tq == 0 and q_off % dh == 0 and k_off % dh == 0 and v_off % dh == 0
    ch = jnp.transpose(c[:, :, :heads], (0, 2, 1))
    cq = ch.reshape(nb, heads, seq, 1)
    ck = ch.reshape(nb, heads, nq, 1, tq)
    qb, kb, vb = q_off // dh, k_off // dh, v_off // dh
    return pl.pallas_call(
        functools.partial(_fox_kernel, tq=tq),
        grid=(nb, heads, nq),
        in_specs=[pl.BlockSpec((1, tq, dh), lambda b, h, i: (b, i, qb + h)),
                  pl.BlockSpec((1, seq, dh), lambda b, h, i: (b, 0, kb + h)),
                  pl.BlockSpec((1, seq, dh), lambda b, h, i: (b, 0, vb + h)),
                  pl.BlockSpec((1, 1, tq, 1), lambda b, h, i: (b, h, i, 0)),
                  pl.BlockSpec((1, 1, nq, 1, tq), lambda b, h, i: (b, h, 0, 0, 0))],
        out_specs=pl.BlockSpec((1, tq, dh), lambda b, h, i: (b, i, h)),
        out_shape=jax.ShapeDtypeStruct((nb, seq, heads * dh), F32),
        compiler_params=_params("parallel", "parallel", "parallel"),
        name="fox_attention",
    )(z3, z3, z3, cq, ck)


def _xattn_kernel(q_ref, k_ref, v_ref, o_ref, *, heads, dh):
    for hd in range(heads):
        sl = slice(hd * dh, (hd + 1) * dh)
        s = lax.dot_general(q_ref[:, sl], k_ref[:, sl], (((1,), (1,)), ((), ())),
                            preferred_element_type=F32)
        s = s - jnp.max(s, axis=-1, keepdims=True)
        p = jnp.exp(s)
        p = p / jnp.sum(p, axis=-1, keepdims=True)
        o_ref[:, sl] = jnp.dot(p.astype(BF16), v_ref[:, sl],
                               preferred_element_type=F32).astype(o_ref.dtype)


def cross_attention(q, k, v, *, nb, heads, tm=512):
    t, w = q.shape
    seq, mlen = t // nb, k.shape[0] // nb
    tm = min(tm, seq)
    nt = seq // tm
    return pl.pallas_call(
        functools.partial(_xattn_kernel, heads=heads, dh=w // heads),
        grid=(nb, nt),
        in_specs=[pl.BlockSpec((tm, w), lambda b, i: (b * nt + i, 0)),
                  pl.BlockSpec((mlen, w), lambda b, i: (b, 0)),
                  pl.BlockSpec((mlen, w), lambda b, i: (b, 0))],
        out_specs=pl.BlockSpec((tm, w), lambda b, i: (b * nt + i, 0)),
        out_shape=jax.ShapeDtypeStruct((t, w), BF16),
        compiler_params=_params("parallel", "parallel"),
        name="cross_attention",
    )(q, k, v)


def kernel(x, mem, ffn1_norm, ffn1_w_gate, ffn1_w_up, ffn1_w_down, mix_norm, w_in, ssm_A_re, ssm_A_im, ssm_log_step, ssm_B_re, ssm_B_im, ssm_C_re, ssm_C_im, ssm_D, ssm_w_glu, ssm_out_norm, fox_b_f, fox_q_norm, fox_k_norm, fox_out_norm, w_out, xattn_norm, mem_norm, xattn_wq, xattn_wk, xattn_wv, xattn_q_norm, xattn_k_norm, xattn_wo, ffn2_norm, ffn2_w_gate, ffn2_w_up, ffn2_w_down, final_norm):
    nb, seq, d = x.shape
    t = nb * seq
    depth = ffn1_norm.shape[0]
    g, n_state = ssm_A_re.shape[1:]
    p = ssm_B_re.shape[-1]
    w_ssm = g * p
    fox_h, fox_dh = fox_b_f.shape[-1], fox_q_norm.shape[-1]
    w_fox = fox_h * fox_dh
    o_q, o_k, o_v, o_f = w_ssm, w_ssm + w_fox, w_ssm + 2 * w_fox, w_ssm + 3 * w_fox
    x_dh = xattn_q_norm.shape[-1]
    x_w = xattn_wq.shape[-1]
    x_heads = x_w // x_dh
    mlen = mem.shape[1]
    tn_in = min(512, w_ssm, w_fox)

    h = x.reshape(t, d)
    mem2 = mem.reshape(nb * mlen, d)

    for l in range(depth):
        h = ffn(h, ffn1_norm[l], ffn1_w_gate[l].astype(BF16), ffn1_w_up[l].astype(BF16),
                (0.5 * ffn1_w_down[l]).astype(BF16))

        hn = rmsnorm(h, mix_norm[l], BF16)
        w_in_b = w_in[l].astype(BF16)
        gain_in = jnp.concatenate([
            jnp.ones((w_ssm,), F32),
            jnp.tile(fox_q_norm[l].astype(F32) * (fox_dh ** -0.5), fox_h),
            jnp.tile(fox_k_norm[l].astype(F32), fox_h),
            jnp.ones((w_fox,), F32)])
        z = matmul(hn, w_in_b, BF16, tn=tn_in, n_cols=o_f, norm_gain=gain_in,
                   norm_group=fox_dh, norm_cols=(o_q, o_v))
        w_f = jnp.pad(w_in_b[:, o_f:], ((0, 0), (0, LANES - fox_h)))
        f_logit = matmul(hn, w_f, F32, tn=LANES)
        z3 = z.reshape(nb, seq, o_f)

        abar_r, abar_i, bbar_r, bbar_i = s5_discretise(
            ssm_A_re[l], ssm_A_im[l], ssm_log_step[l], ssm_B_re[l], ssm_B_im[l])
        y_ssm = s5_core(z3, abar_r, abar_i, bbar_r, bbar_i,
                        ssm_C_re[l].astype(F32), ssm_C_im[l].astype(F32), ssm_D[l])
        y_ssm = glu_norm(y_ssm.reshape(t, w_ssm), ssm_w_glu[l].astype(BF16), ssm_out_norm[l])

        b_f_row = jnp.pad(fox_b_f[l].astype(F32), (0, LANES - fox_h)).reshape(1, LANES)
        c = forget_cumsum(f_logit.reshape(nb, seq, LANES), b_f_row)
        y_fox = fox_attention(z3, c, q_off=o_q, k_off=o_k, v_off=o_v, heads=fox_h, dh=fox_dh)
        y_fox = rmsnorm(y_fox.reshape(t, w_fox), fox_out_norm[l], BF16)

        y = jnp.concatenate([y_ssm, y_fox], axis=-1)
        h = matmul(y, w_out[l].astype(BF16), F32, res=h)

        hn = rmsnorm(h, xattn_norm[l], BF16)
        mn = rmsnorm(mem2, mem_norm[l], BF16)
        q = matmul(hn, xattn_wq[l].astype(BF16), BF16, tn=x_dh,
                   norm_gain=jnp.tile(xattn_q_norm[l].astype(F32) * (x_dh ** -0.5), x_heads),
                   norm_group=x_dh)
        k = matmul(mn, xattn_wk[l].astype(BF16), BF16, tn=x_dh,
                   norm_gain=jnp.tile(xattn_k_norm[l].astype(F32), x_heads), norm_group=x_dh)
        v = matmul(mn, xattn_wv[l].astype(BF16), BF16, tn=x_dh)
        o = cross_attention(q, k, v, nb=nb, heads=x_heads)
        h = matmul(o, xattn_wo[l].astype(BF16), F32, res=h)

        h = ffn(h, ffn2_norm[l], ffn2_w_gate[l].astype(BF16), ffn2_w_up[l].astype(BF16),
                (0.5 * ffn2_w_down[l]).astype(BF16), final_gain=final_norm[l])

    return h.reshape(nb, seq, d)
```

```python
import functools
import math

import jax
import jax.numpy as jnp
from jax import lax
from jax.experimental import pallas as pl
from jax.experimental.pallas import tpu as pltpu

F32 = jnp.float32
BF16 = jnp.bfloat16
EPS = 1e-6
NEG_BIG = -1e30

V7X_VMEM_LIMIT_BYTES = 56 * 1024 * 1024
LANES = 128
SUBLANES = 8


def _params(*sem):
    return pltpu.CompilerParams(dimension_semantics=sem,
                                vmem_limit_bytes=V7X_VMEM_LIMIT_BYTES)


def _rms(x, g):
    ms = jnp.mean(x * x, axis=-1, keepdims=True)
    return x * lax.rsqrt(ms + EPS) * g


def _cast_kernel(w_ref, o_ref, *, scale):
    w = w_ref[...]
    if scale != 1.0:
        w = w * scale
    o_ref[...] = w.astype(o_ref.dtype)


def cast_bf16(w_stack, layer, *, scale=1.0, tr=256):
    _, k, n = w_stack.shape
    tr = min(tr, k)
    assert k % tr == 0
    return pl.pallas_call(
        functools.partial(_cast_kernel, scale=scale),
        grid=(k // tr,),
        in_specs=[pl.BlockSpec((None, tr, n), lambda i: (layer, i, 0))],
        out_specs=pl.BlockSpec((tr, n), lambda i: (i, 0)),
        out_shape=jax.ShapeDtypeStruct((k, n), BF16),
        compiler_params=_params("parallel"),
        name="cast_bf16",
    )(w_stack)


def _rmsnorm_kernel(x_ref, g_ref, *rest):
    o_ref = rest[-1]
    o_ref[...] = _rms(x_ref[...].astype(F32), g_ref[...]).astype(o_ref.dtype)


def rmsnorm(x, g, out_dtype, tm=256, into=None, col_block=0):
    m, k = x.shape
    tm = min(tm, m)
    in_specs = [pl.BlockSpec((tm, k), lambda i: (i, 0)),
                pl.BlockSpec((1, k), lambda i: (0, 0))]
    args = [x, g.reshape(1, k).astype(F32)]
    aliases = {}
    out_shape = jax.ShapeDtypeStruct((m, k), out_dtype)
    if into is not None:
        assert into.shape[0] == m and into.shape[1] % k == 0 and into.dtype == out_dtype
        in_specs.append(pl.BlockSpec(memory_space=pl.ANY))
        args.append(into)
        aliases = {2: 0}
        out_shape = jax.ShapeDtypeStruct(into.shape, into.dtype)
    return pl.pallas_call(
        _rmsnorm_kernel,
        grid=(m // tm,),
        in_specs=in_specs,
        out_specs=pl.BlockSpec((tm, k), lambda i: (i, col_block)),
        out_shape=out_shape,
        input_output_aliases=aliases,
        compiler_params=_params("parallel"),
        name="rmsnorm",
    )(*args)


def _mm_kernel(*refs, has_res, norm_group, norm_lo, norm_hi):
    x_ref, w_ref = refs[0], refs[1]
    pos = 2
    res_ref = g_ref = None
    if has_res:
        res_ref = refs[pos]
        pos += 1
    if norm_group:
        g_ref = refs[pos]
        pos += 1
    o_ref = refs[pos]

    acc = jnp.dot(x_ref[...], w_ref[...], preferred_element_type=F32)
    if has_res:
        acc = acc + res_ref[...]
    if not norm_group:
        o_ref[...] = acc.astype(o_ref.dtype)
        return

    j = pl.program_id(1)
    in_range = jnp.logical_and(j >= norm_lo, j < norm_hi)

    @pl.when(in_range)
    def _():
        for c in range(acc.shape[1] // norm_group):
            sl = slice(c * norm_group, (c + 1) * norm_group)
            o_ref[:, sl] = _rms(acc[:, sl], g_ref[:, sl]).astype(o_ref.dtype)

    @pl.when(jnp.logical_not(in_range))
    def _():
        o_ref[...] = acc.astype(o_ref.dtype)


def matmul(x, w, out_dtype, *, tm=512, tn=512, n_cols=None, res=None,
           norm_gain=None, norm_group=0, norm_cols=None):
    m, k = x.shape
    n = w.shape[1] if n_cols is None else n_cols
    tm, tn = min(tm, m), min(tn, n)
    assert m % tm == 0 and n % tn == 0
    in_specs = [pl.BlockSpec((tm, k), lambda i, j: (i, 0)),
                pl.BlockSpec((k, tn), lambda i, j: (0, j))]
    args = [x, w]
    if res is not None:
        in_specs.append(pl.BlockSpec((tm, tn), lambda i, j: (i, j)))
        args.append(res)
    norm_lo = norm_hi = 0
    if norm_group:
        lo, hi = (0, n) if norm_cols is None else norm_cols
        assert tn % norm_group == 0 and lo % tn == 0 and hi % tn == 0
        norm_lo, norm_hi = lo // tn, hi // tn
        in_specs.append(pl.BlockSpec((1, tn), lambda i, j: (0, j)))
        args.append(norm_gain.reshape(1, n).astype(F32))
    return pl.pallas_call(
        functools.partial(_mm_kernel, has_res=res is not None, norm_group=norm_group,
                          norm_lo=norm_lo, norm_hi=norm_hi),
        grid=(m // tm, n // tn),
        in_specs=in_specs,
        out_specs=pl.BlockSpec((tm, tn), lambda i, j: (i, j)),
        out_shape=jax.ShapeDtypeStruct((m, n), out_dtype),
        compiler_params=_params("parallel", "parallel"),
        name="matmul",
    )(*args)


def _ffn_kernel(*refs, final_norm):
    if final_norm:
        x_ref, g_ref, wg_ref, wu_ref, wd_ref, fg_ref, o_ref, n_scr = refs
    else:
        x_ref, g_ref, wg_ref, wu_ref, wd_ref, o_ref, n_scr = refs
        fg_ref = None
    j = pl.program_id(1)

    @pl.when(j == 0)
    def _():
        x = x_ref[...]
        n_scr[...] = _rms(x, g_ref[...]).astype(BF16)
        o_ref[...] = x

    n = n_scr[...]
    gate = jnp.dot(n, wg_ref[...], preferred_element_type=F32)
    up = jnp.dot(n, wu_ref[...], preferred_element_type=F32)
    hid = (gate * jax.nn.sigmoid(gate) * up).astype(BF16)
    o_ref[...] += jnp.dot(hid, wd_ref[...], preferred_element_type=F32)

    if final_norm:
        @pl.when(j == pl.num_programs(1) - 1)
        def _():
            o_ref[...] = _rms(o_ref[...], fg_ref[...])


def ffn(x, g, wg, wu, wd_half, final_gain=None, *, tm=512, tf=256):
    m, d = x.shape
    dff = wg.shape[1]
    tm, tf = min(tm, m), min(tf, dff)
    assert m % tm == 0 and dff % tf == 0
    in_specs = [pl.BlockSpec((tm, d), lambda i, j: (i, 0), pipeline_mode=pl.Buffered(1)),
                pl.BlockSpec((1, d), lambda i, j: (0, 0)),
                pl.BlockSpec((d, tf), lambda i, j: (0, j)),
                pl.BlockSpec((d, tf), lambda i, j: (0, j)),
                pl.BlockSpec((tf, d), lambda i, j: (j, 0))]
    args = [x, g.reshape(1, d).astype(F32), wg, wu, wd_half]
    if final_gain is not None:
        in_specs.append(pl.BlockSpec((1, d), lambda i, j: (0, 0)))
        args.append(final_gain.reshape(1, d).astype(F32))
    return pl.pallas_call(
        functools.partial(_ffn_kernel, final_norm=final_gain is not None),
        grid=(m // tm, dff // tf),
        in_specs=in_specs,
        out_specs=pl.BlockSpec((tm, d), lambda i, j: (i, 0)),
        out_shape=jax.ShapeDtypeStruct((m, d), F32),
        scratch_shapes=[pltpu.VMEM((tm, d), BF16)],
        compiler_params=_params("parallel", "arbitrary"),
        name="ffn",
    )(*args)


def _s5_disc_kernel(are_ref, aim_ref, ls_ref, bre_ref, bim_ref,
                    abr_ref, abi_ref, bbr_ref, bbi_ref):
    ar, ai = are_ref[...], aim_ref[...]
    dt = jnp.exp(ls_ref[...])
    mag = jnp.exp(ar * dt)
    abar_r = mag * jnp.cos(ai * dt)
    abar_i = mag * jnp.sin(ai * dt)
    den = ar * ar + ai * ai
    pr, pi_ = abar_r - 1.0, abar_i
    coef_r = (pr * ar + pi_ * ai) / den
    coef_i = (pi_ * ar - pr * ai) / den
    br, bi = bre_ref[...], bim_ref[...]
    abr_ref[...] = abar_r
    abi_ref[...] = abar_i
    bbr_ref[...] = coef_r * br - coef_i * bi
    bbi_ref[...] = coef_r * bi + coef_i * br


def s5_discretise(a_re, a_im, log_step, b_re, b_im):
    g, n = a_re.shape
    p = b_re.shape[-1]
    col = jax.ShapeDtypeStruct((g, n, 1), F32)
    mat = jax.ShapeDtypeStruct((g, n, p), F32)
    ls = jnp.broadcast_to(log_step.reshape(g, 1, 1), (g, n, 1))
    abr, abi, bbr, bbi = pl.pallas_call(
        _s5_disc_kernel,
        out_shape=(col, col, mat, mat),
        compiler_params=pltpu.CompilerParams(vmem_limit_bytes=V7X_VMEM_LIMIT_BYTES),
        name="s5_discretise",
    )(a_re.reshape(g, n, 1), a_im.reshape(g, n, 1), ls, b_re, b_im)
    return abr.reshape(g, n), abi.reshape(g, n), bbr, bbi


GROUPS_PER_HALF = 8


def _gelu_tanh(x):
    c = math.sqrt(2.0 / math.pi)
    return 0.5 * x * (1.0 + jnp.tanh(c * (x + 0.044715 * (x * x * x))))


def _s5_kernel(u_ref, rb_ref, rc_ref, ar_ref, ai_ref, d_ref, y_ref,
               lhs_scr, sr_scr, si_scr, o2_scr, xr_scr, xi_scr, *, nb, hs, tt, sw, hw):
    @pl.when(pl.program_id(1) == 0)
    def _():
        xr_scr[...] = jnp.zeros_like(xr_scr)
        xi_scr[...] = jnp.zeros_like(xi_scr)

    zeros = jnp.zeros((tt, hw), F32)
    for b in range(nb):
        ub = u_ref[b].astype(F32)
        for hh in range(hs):
            rows = pl.ds(hh * nb + b, tt, stride=SUBLANES)
            for c in range(hs):
                lhs_scr[c, rows, :] = ub[:, c * hw:(c + 1) * hw] if c == hh else zeros

    lhs = jnp.concatenate([lhs_scr[c] for c in range(hs)], axis=-1).astype(BF16)
    sr_scr[...] = jnp.dot(lhs, rb_ref[0, :, :sw], preferred_element_type=F32)
    si_scr[...] = jnp.dot(lhs, rb_ref[0, :, sw:], preferred_element_type=F32)

    ar, ai = ar_ref[0], ai_ref[0]

    def step(t, carry):
        xr, xi = carry
        rows = pl.ds(pl.multiple_of(t * SUBLANES, SUBLANES), SUBLANES)
        nxr = ar * xr - ai * xi + sr_scr[rows, :]
        nxi = ar * xi + ai * xr + si_scr[rows, :]
        sr_scr[rows, :] = nxr
        si_scr[rows, :] = nxi
        return nxr, nxi

    xr, xi = lax.fori_loop(0, tt, step, (xr_scr[...], xi_scr[...]), unroll=8)
    xr_scr[...] = xr
    xi_scr[...] = xi

    o2 = (jnp.dot(sr_scr[...].astype(BF16), rc_ref[0, :sw, :], preferred_element_type=F32)
          + jnp.dot(si_scr[...].astype(BF16), rc_ref[0, sw:, :], preferred_element_type=F32))
    for c in range(hs):
        o2_scr[c] = o2[:, c * hw:(c + 1) * hw]

    for b in range(nb):
        parts = [o2_scr[hh, pl.ds(hh * nb + b, tt, stride=SUBLANES), :] for hh in range(hs)]
        yb = jnp.concatenate(parts, axis=-1) + d_ref[...] * u_ref[b].astype(F32)
        y_ref[b] = _gelu_tanh(yb).astype(y_ref.dtype)


def s5_core(z3, abar_r, abar_i, bbar_r, bbar_i, c_re, c_im, d_skip, *, tt=256):
    nb, seq, _ = z3.shape
    g, n, p = bbar_r.shape
    assert SUBLANES % nb == 0
    hs = SUBLANES // nb
    gph = GROUPS_PER_HALF
    assert g % (hs * gph) == 0
    nblk = g // (hs * gph)
    hw, sw = gph * p, gph * n
    cw = hs * hw
    assert hw == LANES and sw % LANES == 0
    tt = min(tt, seq)
    assert seq % tt == 0

    eye = jnp.eye(gph, dtype=F32)

    def in_proj(bb):
        bb = bb.reshape(nblk, hs, gph, n, p)
        return jnp.einsum("jhqnp,qr->jhqprn", bb, eye).reshape(nblk, cw, sw)

    def out_proj(cc):
        cc = cc.reshape(nblk, hs, gph, p, n)
        return jnp.einsum("jhrpn,qr->jqnhrp", cc, eye).reshape(nblk, sw, cw)

    rb = jnp.concatenate([in_proj(bbar_r), in_proj(bbar_i)], axis=-1).astype(BF16)
    rc = jnp.concatenate([out_proj(c_re), out_proj(-c_im)], axis=1).astype(BF16)

    def rows(a):
        a = a.reshape(nblk, hs, 1, sw)
        return jnp.broadcast_to(a, (nblk, hs, nb, sw)).reshape(nblk, hs * nb, sw)

    return pl.pallas_call(
        functools.partial(_s5_kernel, nb=nb, hs=hs, tt=tt, sw=sw, hw=hw),
        grid=(nblk, seq // tt),
        in_specs=[pl.BlockSpec((nb, tt, cw), lambda j, t: (0, t, j)),
                  pl.BlockSpec((1, cw, 2 * sw), lambda j, t: (j, 0, 0)),
                  pl.BlockSpec((1, 2 * sw, cw), lambda j, t: (j, 0, 0)),
                  pl.BlockSpec((1, SUBLANES, sw), lambda j, t: (j, 0, 0)),
                  pl.BlockSpec((1, SUBLANES, sw), lambda j, t: (j, 0, 0)),
                  pl.BlockSpec((1, cw), lambda j, t: (0, j))],
        out_specs=pl.BlockSpec((nb, tt, cw), lambda j, t: (0, t, j)),
        out_shape=jax.ShapeDtypeStruct((nb, seq, g * p), BF16),
        scratch_shapes=[pltpu.VMEM((hs, SUBLANES * tt, hw), F32),
                        pltpu.VMEM((SUBLANES * tt, sw), F32),
                        pltpu.VMEM((SUBLANES * tt, sw), F32),
                        pltpu.VMEM((hs, SUBLANES * tt, hw), F32),
                        pltpu.VMEM((SUBLANES, sw), F32),
                        pltpu.VMEM((SUBLANES, sw), F32)],
        compiler_params=_params("parallel", "arbitrary"),
        name="s5_core",
    )(z3, rb, rc, rows(abar_r), rows(abar_i), d_skip.reshape(1, g * p).astype(F32))


def _glu_kernel(y_ref, w_ref, g_ref, o_ref):
    y = y_ref[...]
    v = jnp.dot(y, w_ref[...], preferred_element_type=F32)
    o = y.astype(F32) * jax.nn.sigmoid(v)
    o_ref[...] = _rms(o, g_ref[...]).astype(o_ref.dtype)


def glu_norm(y, w, g, *, out_cols, tm=512):
    m, k = y.shape
    tm = min(tm, m)
    return pl.pallas_call(
        _glu_kernel,
        grid=(m // tm,),
        in_specs=[pl.BlockSpec((tm, k), lambda i: (i, 0)),
                  pl.BlockSpec((k, k), lambda i: (0, 0)),
                  pl.BlockSpec((1, k), lambda i: (0, 0))],
        out_specs=pl.BlockSpec((tm, k), lambda i: (i, 0)),
        out_shape=jax.ShapeDtypeStruct((m, out_cols), BF16),
        compiler_params=_params("parallel"),
        name="glu_norm",
    )(y, w, g.reshape(1, k).astype(F32))


LOG2E = math.log2(math.e)
def _cumsum_kernel(f_ref, bf_ref, c_ref, *, chunk):
    seq = f_ref.shape[1]
    row = lax.broadcasted_iota(jnp.int32, (chunk, chunk), 0)
    col = lax.broadcasted_iota(jnp.int32, (chunk, chunk), 1)
    tri = (col <= row).astype(F32)
    carry = jnp.zeros((1, f_ref.shape[2]), F32)
    for ci in range(seq // chunk):
        sl = slice(ci * chunk, (ci + 1) * chunk)
        x = f_ref[0, sl, :] + bf_ref[...]
        log_f = (jnp.minimum(x, 0.0) - jnp.log1p(jnp.exp(-jnp.abs(x)))) * LOG2E
        cs = jnp.dot(tri, log_f, precision=lax.Precision.HIGHEST,
                     preferred_element_type=F32) + carry
        c_ref[0, sl, :] = cs
        carry = cs[chunk - 1:chunk, :]


def forget_cumsum(f3, b_f_row, *, chunk=256):
    nb, seq, w = f3.shape
    chunk = min(chunk, seq)
    return pl.pallas_call(
        functools.partial(_cumsum_kernel, chunk=chunk),
        grid=(nb,),
        in_specs=[pl.BlockSpec((1, seq, w), lambda b: (b, 0, 0)),
                  pl.BlockSpec((1, w), lambda b: (0, 0))],
        out_specs=pl.BlockSpec((1, seq, w), lambda b: (b, 0, 0)),
        out_shape=jax.ShapeDtypeStruct((nb, seq, w), F32),
        compiler_params=_params("parallel"),
        name="forget_cumsum",
    )(f3, b_f_row)


def _fox_kernel(q_ref, k_ref, v_ref, cq_ref, ck_ref, o_ref, *, tq, big):
    i = pl.program_id(2)
    q = q_ref[0]
    cq = cq_ref[0, 0]
    per_big = big // tq

    def block(start, width, carry, diagonal):
        m, l, acc = carry
        rows = pl.ds(start, width)
        k = k_ref[0, rows, :]
        v = v_ref[0, rows, :]
        s = lax.dot_general(q, k, (((1,), (1,)), ((), ())), preferred_element_type=F32)
        s = s + (cq - ck_ref[0, 0, :, rows])
        if diagonal:
            r = lax.broadcasted_iota(jnp.int32, s.shape, 0)
            c = lax.broadcasted_iota(jnp.int32, s.shape, 1)
            s = jnp.where(c <= r, s, NEG_BIG)
        m_new = jnp.maximum(m, jnp.max(s, axis=-1, keepdims=True))
        alpha = jnp.exp2(m - m_new)
        p = jnp.exp2(s - m_new)
        l = alpha * l + jnp.sum(p, axis=-1, keepdims=True)
        acc = alpha * acc + jnp.dot(p.astype(v.dtype), v, preferred_element_type=F32)
        return m_new, l, acc

    dh = q.shape[-1]
    init = (jnp.full((tq, 1), NEG_BIG, F32), jnp.zeros((tq, 1), F32),
            jnp.zeros((tq, dh), F32))
    n_big = i // per_big
    carry = lax.fori_loop(
        0, n_big, lambda j, c: block(pl.multiple_of(j * big, big), big, c, False), init)
    carry = lax.fori_loop(
        n_big * per_big, i, lambda j, c: block(pl.multiple_of(j * tq, tq), tq, c, False), carry)
    _, l, acc = block(pl.multiple_of(i * tq, tq), tq, carry, True)
    o_ref[0] = (acc / l).astype(o_ref.dtype)


def fox_attention(z3, c, *, q_off, k_off, v_off, heads, dh, tq=512, big=1024):
    nb, seq, _ = z3.shape
    tq = min(tq, seq)
    big = min(big, seq)
    nq = seq // tq
    assert seq % tq == 0 and big % tq == 0
    assert q_off % dh == 0 and k_off % dh == 0 and v_off % dh == 0
    ch = jnp.transpose(c[:, :, :heads], (0, 2, 1))
    cq = ch.reshape(nb, heads, seq, 1)
    ck = ch.reshape(nb, heads, 1, seq)
    qb, kb, vb = q_off // dh, k_off // dh, v_off // dh
    return pl.pallas_call(
        functools.partial(_fox_kernel, tq=tq, big=big),
        grid=(nb, heads, nq),
        in_specs=[pl.BlockSpec((1, tq, dh), lambda b, h, i: (b, i, qb + h)),
                  pl.BlockSpec((1, seq, dh), lambda b, h, i: (b, 0, kb + h)),
                  pl.BlockSpec((1, seq, dh), lambda b, h, i: (b, 0, vb + h)),
                  pl.BlockSpec((1, 1, tq, 1), lambda b, h, i: (b, h, i, 0)),
                  pl.BlockSpec((1, 1, 1, seq), lambda b, h, i: (b, h, 0, 0))],
        out_specs=pl.BlockSpec((1, tq, dh), lambda b, h, i: (b, i, h)),
        out_shape=jax.ShapeDtypeStruct((nb, seq, heads * dh), F32),
        compiler_params=_params("parallel", "parallel", "parallel"),
        name="fox_attention",
    )(z3, z3, z3, cq, ck)


def _xattn_kernel(q_ref, k_ref, v_ref, o_ref, *, heads, dh):
    for hd in range(heads):
        sl = slice(hd * dh, (hd + 1) * dh)
        s = lax.dot_general(q_ref[:, sl], k_ref[:, sl], (((1,), (1,)), ((), ())),
                            preferred_element_type=F32)
        s = s - jnp.max(s, axis=-1, keepdims=True)
        p = jnp.exp(s)
        p = p / jnp.sum(p, axis=-1, keepdims=True)
        o_ref[:, sl] = jnp.dot(p.astype(BF16), v_ref[:, sl],
                               preferred_element_type=F32).astype(o_ref.dtype)


def cross_attention(q, k, v, *, nb, heads, tm=512):
    t, w = q.shape
    seq, mlen = t // nb, k.shape[0] // nb
    tm = min(tm, seq)
    nt = seq // tm
    return pl.pallas_call(
        functools.partial(_xattn_kernel, heads=heads, dh=w // heads),
        grid=(nb, nt),
        in_specs=[pl.BlockSpec((tm, w), lambda b, i: (b * nt + i, 0)),
                  pl.BlockSpec((mlen, w), lambda b, i: (b, 0)),
                  pl.BlockSpec((mlen, w), lambda b, i: (b, 0))],
        out_specs=pl.BlockSpec((tm, w), lambda b, i: (b * nt + i, 0)),
        out_shape=jax.ShapeDtypeStruct((t, w), BF16),
        compiler_params=_params("parallel", "parallel"),
        name="cross_attention",
    )(q, k, v)


def kernel(x, mem, ffn1_norm, ffn1_w_gate, ffn1_w_up, ffn1_w_down, mix_norm, w_in, ssm_A_re, ssm_A_im, ssm_log_step, ssm_B_re, ssm_B_im, ssm_C_re, ssm_C_im, ssm_D, ssm_w_glu, ssm_out_norm, fox_b_f, fox_q_norm, fox_k_norm, fox_out_norm, w_out, xattn_norm, mem_norm, xattn_wq, xattn_wk, xattn_wv, xattn_q_norm, xattn_k_norm, xattn_wo, ffn2_norm, ffn2_w_gate, ffn2_w_up, ffn2_w_down, final_norm):
    nb, seq, d = x.shape
    t = nb * seq
    depth = ffn1_norm.shape[0]
    g, n_state = ssm_A_re.shape[1:]
    p = ssm_B_re.shape[-1]
    w_ssm = g * p
    fox_h, fox_dh = fox_b_f.shape[-1], fox_q_norm.shape[-1]
    w_fox = fox_h * fox_dh
    o_q, o_k, o_v, o_f = w_ssm, w_ssm + w_fox, w_ssm + 2 * w_fox, w_ssm + 3 * w_fox
    x_dh = xattn_q_norm.shape[-1]
    x_w = xattn_wq.shape[-1]
    x_heads = x_w // x_dh
    mlen = mem.shape[1]
    tn_in = min(1024, w_ssm, w_fox)
    assert w_ssm % w_fox == 0

    h = x.reshape(t, d)
    mem2 = mem.reshape(nb * mlen, d)

    for l in range(depth):
        h = ffn(h, ffn1_norm[l], cast_bf16(ffn1_w_gate, l), cast_bf16(ffn1_w_up, l),
                cast_bf16(ffn1_w_down, l, scale=0.5))

        hn = rmsnorm(h, mix_norm[l], BF16)
        w_in_b = cast_bf16(w_in, l)
        gain_in = jnp.concatenate([
            jnp.ones((w_ssm,), F32),
            jnp.tile(fox_q_norm[l].astype(F32) * (fox_dh ** -0.5 * LOG2E), fox_h),
            jnp.tile(fox_k_norm[l].astype(F32), fox_h),
            jnp.ones((w_fox,), F32)])
        z = matmul(hn, w_in_b, BF16, tm=1024, tn=tn_in, n_cols=o_f, norm_gain=gain_in,
                   norm_group=fox_dh, norm_cols=(o_q, o_v))
        w_f = jnp.pad(w_in_b[:, o_f:], ((0, 0), (0, LANES - fox_h)))
        f_logit = matmul(hn, w_f, F32, tm=1024, tn=LANES)
        z3 = z.reshape(nb, seq, o_f)

        abar_r, abar_i, bbar_r, bbar_i = s5_discretise(
            ssm_A_re[l], ssm_A_im[l], ssm_log_step[l], ssm_B_re[l], ssm_B_im[l])
        y_ssm = s5_core(z3, abar_r, abar_i, bbar_r, bbar_i,
                        ssm_C_re[l].astype(F32), ssm_C_im[l].astype(F32), ssm_D[l])
        y = glu_norm(y_ssm.reshape(t, w_ssm), cast_bf16(ssm_w_glu, l), ssm_out_norm[l],
                     out_cols=w_ssm + w_fox)

        b_f_row = jnp.pad(fox_b_f[l].astype(F32), (0, LANES - fox_h)).reshape(1, LANES)
        c = forget_cumsum(f_logit.reshape(nb, seq, LANES), b_f_row)
        y_fox = fox_attention(z3, c, q_off=o_q, k_off=o_k, v_off=o_v, heads=fox_h, dh=fox_dh)
        y = rmsnorm(y_fox.reshape(t, w_fox), fox_out_norm[l], BF16, into=y,
                    col_block=w_ssm // w_fox)

        h = matmul(y, cast_bf16(w_out, l), F32, tm=1024, tn=512, res=h)

        hn = rmsnorm(h, xattn_norm[l], BF16)
        mn = rmsnorm(mem2, mem_norm[l], BF16)
        q = matmul(hn, cast_bf16(xattn_wq, l), BF16, tm=1024, tn=x_dh,
                   norm_gain=jnp.tile(xattn_q_norm[l].astype(F32) * (x_dh ** -0.5), x_heads),
                   norm_group=x_dh)
        k = matmul(mn, cast_bf16(xattn_wk, l), BF16, tm=1024, tn=x_dh,
                   norm_gain=jnp.tile(xattn_k_norm[l].astype(F32), x_heads), norm_group=x_dh)
        v = matmul(mn, cast_bf16(xattn_wv, l), BF16, tm=1024, tn=x_dh)
        o = cross_attention(q, k, v, nb=nb, heads=x_heads)
        h = matmul(o, cast_bf16(xattn_wo, l), F32, tm=1024, tn=512, res=h)

        h = ffn(h, ffn2_norm[l], cast_bf16(ffn2_w_gate, l), cast_bf16(ffn2_w_up, l),
                cast_bf16(ffn2_w_down, l, scale=0.5), final_gain=final_norm[l])

    return h.reshape(nb, seq, d)
```

```python
import functools
import math

import jax
import jax.numpy as jnp
from jax import lax
from jax.experimental import pallas as pl
from jax.experimental.pallas import tpu as pltpu

F32 = jnp.float32
BF16 = jnp.bfloat16
EPS = 1e-6
NEG_BIG = -1e30

V7X_VMEM_LIMIT_BYTES = 58 * 1024 * 1024
LANES = 128
SUBLANES = 8


def _params(*sem):
    return pltpu.CompilerParams(dimension_semantics=sem,
                                vmem_limit_bytes=V7X_VMEM_LIMIT_BYTES)


def _rms(x, g):
    ms = jnp.mean(x * x, axis=-1, keepdims=True)
    return x * lax.rsqrt(ms + EPS) * g


def _cast_kernel(w_ref, o_ref, *, scale):
    w = w_ref[...]
    if scale != 1.0:
        w = w * scale
    o_ref[...] = w.astype(o_ref.dtype)


def cast_bf16(w_stack, layer, *, scale=1.0, tr=256):
    _, k, n = w_stack.shape
    tr = min(tr, k)
    assert k % tr == 0
    return pl.pallas_call(
        functools.partial(_cast_kernel, scale=scale),
        grid=(k // tr,),
        in_specs=[pl.BlockSpec((None, tr, n), lambda i: (layer, i, 0))],
        out_specs=pl.BlockSpec((tr, n), lambda i: (i, 0)),
        out_shape=jax.ShapeDtypeStruct((k, n), BF16),
        compiler_params=_params("parallel"),
        name="cast_bf16",
    )(w_stack)


def cast_bf16_tiles(w_stack, layer, tn):
    _, k, n = w_stack.shape
    tn = min(tn, n)
    assert n % tn == 0
    return pl.pallas_call(
        functools.partial(_cast_kernel, scale=1.0),
        grid=(n // tn,),
        in_specs=[pl.BlockSpec((None, k, tn), lambda j: (layer, 0, j))],
        out_specs=pl.BlockSpec((None, k, tn), lambda j: (j, 0, 0)),
        out_shape=jax.ShapeDtypeStruct((n // tn, k, tn), BF16),
        compiler_params=_params("parallel"),
        name="cast_bf16_tiles",
    )(w_stack)


def _rmsnorm_kernel(x_ref, g_ref, *rest):
    o_ref = rest[-1]
    o_ref[...] = _rms(x_ref[...].astype(F32), g_ref[...]).astype(o_ref.dtype)


def rmsnorm(x, g, out_dtype, tm=256, into=None, col_block=0):
    m, k = x.shape
    tm = min(tm, m)
    in_specs = [pl.BlockSpec((tm, k), lambda i: (i, 0)),
                pl.BlockSpec((1, k), lambda i: (0, 0))]
    args = [x, g.reshape(1, k).astype(F32)]
    aliases = {}
    out_shape = jax.ShapeDtypeStruct((m, k), out_dtype)
    if into is not None:
        assert into.shape[0] == m and into.shape[1] % k == 0 and into.dtype == out_dtype
        in_specs.append(pl.BlockSpec(memory_space=pl.ANY))
        args.append(into)
        aliases = {2: 0}
        out_shape = jax.ShapeDtypeStruct(into.shape, into.dtype)
    return pl.pallas_call(
        _rmsnorm_kernel,
        grid=(m // tm,),
        in_specs=in_specs,
        out_specs=pl.BlockSpec((tm, k), lambda i: (i, col_block)),
        out_shape=out_shape,
        input_output_aliases=aliases,
        compiler_params=_params("parallel"),
        name="rmsnorm",
    )(*args)


def _mm_kernel(*refs, has_res, norm_group, norm_lo, norm_hi):
    x_ref, w_ref = refs[0], refs[1]
    pos = 2
    res_ref = g_ref = None
    if has_res:
        res_ref = refs[pos]
        pos += 1
    if norm_group:
        g_ref = refs[pos]
        pos += 1
    o_ref = refs[pos]

    acc = jnp.dot(x_ref[...], w_ref[...], preferred_element_type=F32)
    if has_res:
        acc = acc + res_ref[...]
    if not norm_group:
        o_ref[...] = acc.astype(o_ref.dtype)
        return

    j = pl.program_id(1)
    in_range = jnp.logical_and(j >= norm_lo, j < norm_hi)

    @pl.when(in_range)
    def _():
        for c in range(acc.shape[1] // norm_group):
            sl = slice(c * norm_group, (c + 1) * norm_group)
            o_ref[:, sl] = _rms(acc[:, sl], g_ref[:, sl]).astype(o_ref.dtype)

    @pl.when(jnp.logical_not(in_range))
    def _():
        o_ref[...] = acc.astype(o_ref.dtype)


def matmul(x, w, out_dtype, *, tm=512, tn=512, n_cols=None, res=None,
           norm_gain=None, norm_group=0, norm_cols=None):
    m, k = x.shape
    n = w.shape[1] if n_cols is None else n_cols
    tm, tn = min(tm, m), min(tn, n)
    assert m % tm == 0 and n % tn == 0
    in_specs = [pl.BlockSpec((tm, k), lambda i, j: (i, 0)),
                pl.BlockSpec((k, tn), lambda i, j: (0, j))]
    args = [x, w]
    if res is not None:
        in_specs.append(pl.BlockSpec((tm, tn), lambda i, j: (i, j)))
        args.append(res)
    norm_lo = norm_hi = 0
    if norm_group:
        lo, hi = (0, n) if norm_cols is None else norm_cols
        assert tn % norm_group == 0 and lo % tn == 0 and hi % tn == 0
        norm_lo, norm_hi = lo // tn, hi // tn
        in_specs.append(pl.BlockSpec((1, tn), lambda i, j: (0, j)))
        args.append(norm_gain.reshape(1, n).astype(F32))
    return pl.pallas_call(
        functools.partial(_mm_kernel, has_res=res is not None, norm_group=norm_group,
                          norm_lo=norm_lo, norm_hi=norm_hi),
        grid=(m // tm, n // tn),
        in_specs=in_specs,
        out_specs=pl.BlockSpec((tm, tn), lambda i, j: (i, j)),
        out_shape=jax.ShapeDtypeStruct((m, n), out_dtype),
        compiler_params=_params("parallel", "parallel"),
        name="matmul",
    )(*args)


def _ffn_kernel(*refs, final_norm):
    if final_norm:
        x_ref, g_ref, wg_ref, wu_ref, wd_ref, fg_ref, o_ref, n_scr = refs
    else:
        x_ref, g_ref, wg_ref, wu_ref, wd_ref, o_ref, n_scr = refs
        fg_ref = None
    j = pl.program_id(1)

    @pl.when(j == 0)
    def _():
        x = x_ref[...]
        n_scr[...] = _rms(x, g_ref[...]).astype(BF16)
        o_ref[...] = x

    n = n_scr[...]
    gate = jnp.dot(n, wg_ref[...], preferred_element_type=F32)
    up = jnp.dot(n, wu_ref[...], preferred_element_type=F32)
    hid = (gate * jax.nn.sigmoid(gate) * up).astype(BF16)
    o_ref[...] += jnp.dot(hid, wd_ref[...], preferred_element_type=F32)

    if final_norm:
        @pl.when(j == pl.num_programs(1) - 1)
        def _():
            o_ref[...] = _rms(o_ref[...], fg_ref[...])


def ffn(x, g, wg, wu, wd_half, final_gain=None, *, tm=512, tf=256, x_buffers=1):
    m, d = x.shape
    tile_major = wg.ndim == 3
    dff = wd_half.shape[0]
    tm, tf = min(tm, m), min(tf, dff)
    assert m % tm == 0 and dff % tf == 0
    if tile_major:
        assert wg.shape == (dff // tf, d, tf) and wu.shape == wg.shape
        w_spec = pl.BlockSpec((None, d, tf), lambda i, j: (j, 0, 0))
    else:
        w_spec = pl.BlockSpec((d, tf), lambda i, j: (0, j))
    in_specs = [pl.BlockSpec((tm, d), lambda i, j: (i, 0), pipeline_mode=pl.Buffered(x_buffers)),
                pl.BlockSpec((1, d), lambda i, j: (0, 0)),
                w_spec,
                w_spec,
                pl.BlockSpec((tf, d), lambda i, j: (j, 0))]
    args = [x, g.reshape(1, d).astype(F32), wg, wu, wd_half]
    if final_gain is not None:
        in_specs.append(pl.BlockSpec((1, d), lambda i, j: (0, 0)))
        args.append(final_gain.reshape(1, d).astype(F32))
    return pl.pallas_call(
        functools.partial(_ffn_kernel, final_norm=final_gain is not None),
        grid=(m // tm, dff // tf),
        in_specs=in_specs,
        out_specs=pl.BlockSpec((tm, d), lambda i, j: (i, 0)),
        out_shape=jax.ShapeDtypeStruct((m, d), F32),
        scratch_shapes=[pltpu.VMEM((tm, d), BF16)],
        compiler_params=_params("parallel", "arbitrary"),
        name="ffn",
    )(*args)


def _s5_disc_kernel(are_ref, aim_ref, ls_ref, bre_ref, bim_ref,
                    abr_ref, abi_ref, bbr_ref, bbi_ref):
    ar, ai = are_ref[...], aim_ref[...]
    dt = jnp.exp(ls_ref[...])
    mag = jnp.exp(ar * dt)
    abar_r = mag * jnp.cos(ai * dt)
    abar_i = mag * jnp.sin(ai * dt)
    den = ar * ar + ai * ai
    pr, pi_ = abar_r - 1.0, abar_i
    coef_r = (pr * ar + pi_ * ai) / den
    coef_i = (pi_ * ar - pr * ai) / den
    br, bi = bre_ref[...], bim_ref[...]
    abr_ref[...] = abar_r
    abi_ref[...] = abar_i
    bbr_ref[...] = coef_r * br - coef_i * bi
    bbi_ref[...] = coef_r * bi + coef_i * br


def s5_discretise(a_re, a_im, log_step, b_re, b_im):
    g, n = a_re.shape
    p = b_re.shape[-1]
    col = jax.ShapeDtypeStruct((g, n, 1), F32)
    mat = jax.ShapeDtypeStruct((g, n, p), F32)
    ls = jnp.broadcast_to(log_step.reshape(g, 1, 1), (g, n, 1))
    abr, abi, bbr, bbi = pl.pallas_call(
        _s5_disc_kernel,
        out_shape=(col, col, mat, mat),
        compiler_params=pltpu.CompilerParams(vmem_limit_bytes=V7X_VMEM_LIMIT_BYTES),
        name="s5_discretise",
    )(a_re.reshape(g, n, 1), a_im.reshape(g, n, 1), ls, b_re, b_im)
    return abr.reshape(g, n), abi.reshape(g, n), bbr, bbi


GROUPS_PER_HALF = 8


def _gelu_tanh(x):
    c = math.sqrt(2.0 / math.pi)
    return 0.5 * x * (1.0 + jnp.tanh(c * (x + 0.044715 * (x * x * x))))


def _s5_kernel(u_ref, rb_ref, rc_ref, ar_ref, ai_ref, d_ref, y_ref,
               lhs_scr, sr_scr, si_scr, o2_scr, xr_scr, xi_scr, *, nb, hs, tt, sw, hw):
    @pl.when(pl.program_id(1) == 0)
    def _():
        xr_scr[...] = jnp.zeros_like(xr_scr)
        xi_scr[...] = jnp.zeros_like(xi_scr)

    zeros = jnp.zeros((tt, hw), F32)
    for b in range(nb):
        ub = u_ref[b].astype(F32)
        for hh in range(hs):
            rows = pl.ds(hh * nb + b, tt, stride=SUBLANES)
            for c in range(hs):
                lhs_scr[c, rows, :] = ub[:, c * hw:(c + 1) * hw] if c == hh else zeros

    lhs = jnp.concatenate([lhs_scr[c] for c in range(hs)], axis=-1).astype(BF16)
    sr_scr[...] = jnp.dot(lhs, rb_ref[0, :, :sw], preferred_element_type=F32)
    si_scr[...] = jnp.dot(lhs, rb_ref[0, :, sw:], preferred_element_type=F32)

    ar, ai = ar_ref[0], ai_ref[0]

    def step(t, carry):
        xr, xi = carry
        rows = pl.ds(pl.multiple_of(t * SUBLANES, SUBLANES), SUBLANES)
        nxr = ar * xr - ai * xi + sr_scr[rows, :]
        nxi = ar * xi + ai * xr + si_scr[rows, :]
        sr_scr[rows, :] = nxr
        si_scr[rows, :] = nxi
        return nxr, nxi

    xr, xi = lax.fori_loop(0, tt, step, (xr_scr[...], xi_scr[...]), unroll=8)
    xr_scr[...] = xr
    xi_scr[...] = xi

    o2 = (jnp.dot(sr_scr[...].astype(BF16), rc_ref[0, :sw, :], preferred_element_type=F32)
          + jnp.dot(si_scr[...].astype(BF16), rc_ref[0, sw:, :], preferred_element_type=F32))
    for c in range(hs):
        o2_scr[c] = o2[:, c * hw:(c + 1) * hw]

    for b in range(nb):
        parts = [o2_scr[hh, pl.ds(hh * nb + b, tt, stride=SUBLANES), :] for hh in range(hs)]
        yb = jnp.concatenate(parts, axis=-1) + d_ref[...] * u_ref[b].astype(F32)
        y_ref[b] = _gelu_tanh(yb).astype(y_ref.dtype)


def s5_core(z3, abar_r, abar_i, bbar_r, bbar_i, c_re, c_im, d_skip, *, tt=256):
    nb, seq, _ = z3.shape
    g, n, p = bbar_r.shape
    assert SUBLANES % nb == 0
    hs = SUBLANES // nb
    gph = GROUPS_PER_HALF
    assert g % (hs * gph) == 0
    nblk = g // (hs * gph)
    hw, sw = gph * p, gph * n
    cw = hs * hw
    assert hw == LANES and sw % LANES == 0
    tt = min(tt, seq)
    assert seq % tt == 0

    eye = jnp.eye(gph, dtype=F32)

    def in_proj(bb):
        bb = bb.reshape(nblk, hs, gph, n, p)
        return jnp.einsum("jhqnp,qr->jhqprn", bb, eye).reshape(nblk, cw, sw)

    def out_proj(cc):
        cc = cc.reshape(nblk, hs, gph, p, n)
        return jnp.einsum("jhrpn,qr->jqnhrp", cc, eye).reshape(nblk, sw, cw)

    rb = jnp.concatenate([in_proj(bbar_r), in_proj(bbar_i)], axis=-1).astype(BF16)
    rc = jnp.concatenate([out_proj(c_re), out_proj(-c_im)], axis=1).astype(BF16)

    def rows(a):
        a = a.reshape(nblk, hs, 1, sw)
        return jnp.broadcast_to(a, (nblk, hs, nb, sw)).reshape(nblk, hs * nb, sw)

    return pl.pallas_call(
        functools.partial(_s5_kernel, nb=nb, hs=hs, tt=tt, sw=sw, hw=hw),
        grid=(nblk, seq // tt),
        in_specs=[pl.BlockSpec((nb, tt, cw), lambda j, t: (0, t, j)),
                  pl.BlockSpec((1, cw, 2 * sw), lambda j, t: (j, 0, 0)),
                  pl.BlockSpec((1, 2 * sw, cw), lambda j, t: (j, 0, 0)),
                  pl.BlockSpec((1, SUBLANES, sw), lambda j, t: (j, 0, 0)),
                  pl.BlockSpec((1, SUBLANES, sw), lambda j, t: (j, 0, 0)),
                  pl.BlockSpec((1, cw), lambda j, t: (0, j))],
        out_specs=pl.BlockSpec((nb, tt, cw), lambda j, t: (0, t, j)),
        out_shape=jax.ShapeDtypeStruct((nb, seq, g * p), BF16),
        scratch_shapes=[pltpu.VMEM((hs, SUBLANES * tt, hw), F32),
                        pltpu.VMEM((SUBLANES * tt, sw), F32),
                        pltpu.VMEM((SUBLANES * tt, sw), F32),
                        pltpu.VMEM((hs, SUBLANES * tt, hw), F32),
                        pltpu.VMEM((SUBLANES, sw), F32),
                        pltpu.VMEM((SUBLANES, sw), F32)],
        compiler_params=_params("parallel", "arbitrary"),
        name="s5_core",
    )(z3, rb, rc, rows(abar_r), rows(abar_i), d_skip.reshape(1, g * p).astype(F32))


def _glu_kernel(y_ref, w_ref, g_ref, o_ref):
    y = y_ref[...]
    v = jnp.dot(y, w_ref[...], preferred_element_type=F32)
    o = y.astype(F32) * jax.nn.sigmoid(v)
    o_ref[...] = _rms(o, g_ref[...]).astype(o_ref.dtype)


def glu_norm(y, w, g, *, out_cols, tm=512):
    m, k = y.shape
    tm = min(tm, m)
    return pl.pallas_call(
        _glu_kernel,
        grid=(m // tm,),
        in_specs=[pl.BlockSpec((tm, k), lambda i: (i, 0)),
                  pl.BlockSpec((k, k), lambda i: (0, 0)),
                  pl.BlockSpec((1, k), lambda i: (0, 0))],
        out_specs=pl.BlockSpec((tm, k), lambda i: (i, 0)),
        out_shape=jax.ShapeDtypeStruct((m, out_cols), BF16),
        compiler_params=_params("parallel"),
        name="glu_norm",
    )(y, w, g.reshape(1, k).astype(F32))


LOG2E = math.log2(math.e)
def _cumsum_kernel(f_ref, bf_ref, c_ref, *, chunk):
    seq = f_ref.shape[1]
    row = lax.broadcasted_iota(jnp.int32, (chunk, chunk), 0)
    col = lax.broadcasted_iota(jnp.int32, (chunk, chunk), 1)
    tri = (col <= row).astype(F32)
    carry = jnp.zeros((1, f_ref.shape[2]), F32)
    for ci in range(seq // chunk):
        sl = slice(ci * chunk, (ci + 1) * chunk)
        x = f_ref[0, sl, :] + bf_ref[...]
        log_f = (jnp.minimum(x, 0.0) - jnp.log1p(jnp.exp(-jnp.abs(x)))) * LOG2E
        cs = jnp.dot(tri, log_f, precision=lax.Precision.HIGHEST,
                     preferred_element_type=F32) + carry
        c_ref[0, sl, :] = cs
        carry = cs[chunk - 1:chunk, :]


def forget_cumsum(f3, b_f_row, *, chunk=256):
    nb, seq, w = f3.shape
    chunk = min(chunk, seq)
    return pl.pallas_call(
        functools.partial(_cumsum_kernel, chunk=chunk),
        grid=(nb,),
        in_specs=[pl.BlockSpec((1, seq, w), lambda b: (b, 0, 0)),
                  pl.BlockSpec((1, w), lambda b: (0, 0))],
        out_specs=pl.BlockSpec((1, seq, w), lambda b: (b, 0, 0)),
        out_shape=jax.ShapeDtypeStruct((nb, seq, w), F32),
        compiler_params=_params("parallel"),
        name="forget_cumsum",
    )(f3, b_f_row)


def _fox_kernel(q_ref, k_ref, v_ref, cq_ref, ck_ref, o_ref, *, tq, big):
    i = pl.program_id(2)
    q = q_ref[0]
    cq = cq_ref[0, 0]
    per_big = big // tq

    def block(start, width, carry, diagonal):
        m, l, acc = carry
        rows = pl.ds(start, width)
        k = k_ref[0, rows, :]
        v = v_ref[0, rows, :]
        s = lax.dot_general(q, k, (((1,), (1,)), ((), ())), preferred_element_type=F32)
        s = s + (cq - ck_ref[0, 0, :, rows])
        if diagonal:
            r = lax.broadcasted_iota(jnp.int32, s.shape, 0)
            c = lax.broadcasted_iota(jnp.int32, s.shape, 1)
            s = jnp.where(c <= r, s, NEG_BIG)
        m_new = jnp.maximum(m, jnp.max(s, axis=-1, keepdims=True))
        alpha = jnp.exp2(m - m_new)
        p = jnp.exp2(s - m_new)
        l = alpha * l + jnp.sum(p, axis=-1, keepdims=True)
        acc = alpha * acc + jnp.dot(p.astype(v.dtype), v, preferred_element_type=F32)
        return m_new, l, acc

    dh = q.shape[-1]
    init = (jnp.full((tq, 1), NEG_BIG, F32), jnp.zeros((tq, 1), F32),
            jnp.zeros((tq, dh), F32))
    n_big = i // per_big
    carry = lax.fori_loop(
        0, n_big, lambda j, c: block(pl.multiple_of(j * big, big), big, c, False), init)
    carry = lax.fori_loop(
        n_big * per_big, i, lambda j, c: block(pl.multiple_of(j * tq, tq), tq, c, False), carry)
    _, l, acc = block(pl.multiple_of(i * tq, tq), tq, carry, True)
    o_ref[0] = (acc / l).astype(o_ref.dtype)


def fox_attention(z3, c, *, q_off, k_off, v_off, heads, dh, tq=512, big=1024):
    nb, seq, _ = z3.shape
    tq = min(tq, seq)
    big = min(big, seq)
    nq = seq // tq
    assert seq % tq == 0 and big % tq == 0
    assert q_off % dh == 0 and k_off % dh == 0 and v_off % dh == 0
    ch = jnp.transpose(c[:, :, :heads], (0, 2, 1))
    cq = ch.reshape(nb, heads, seq, 1)
    ck = ch.reshape(nb, heads, 1, seq)
    qb, kb, vb = q_off // dh, k_off // dh, v_off // dh
    return pl.pallas_call(
        functools.partial(_fox_kernel, tq=tq, big=big),
        grid=(nb, heads, nq),
        in_specs=[pl.BlockSpec((1, tq, dh), lambda b, h, i: (b, i, qb + h)),
                  pl.BlockSpec((1, seq, dh), lambda b, h, i: (b, 0, kb + h)),
                  pl.BlockSpec((1, seq, dh), lambda b, h, i: (b, 0, vb + h)),
                  pl.BlockSpec((1, 1, tq, 1), lambda b, h, i: (b, h, i, 0)),
                  pl.BlockSpec((1, 1, 1, seq), lambda b, h, i: (b, h, 0, 0))],
        out_specs=pl.BlockSpec((1, tq, dh), lambda b, h, i: (b, i, h)),
        out_shape=jax.ShapeDtypeStruct((nb, seq, heads * dh), F32),
        compiler_params=_params("parallel", "parallel", "parallel"),
        name="fox_attention",
    )(z3, z3, z3, cq, ck)


def _xattn_kernel(q_ref, k_ref, v_ref, o_ref, *, heads, dh):
    for hd in range(heads):
        sl = slice(hd * dh, (hd + 1) * dh)
        s = lax.dot_general(q_ref[:, sl], k_ref[:, sl], (((1,), (1,)), ((), ())),
                            preferred_element_type=F32)
        s = s - jnp.max(s, axis=-1, keepdims=True)
        p = jnp.exp(s)
        p = p / jnp.sum(p, axis=-1, keepdims=True)
        o_ref[:, sl] = jnp.dot(p.astype(BF16), v_ref[:, sl],
                               preferred_element_type=F32).astype(o_ref.dtype)


def cross_attention(q, k, v, *, nb, heads, tm=512):
    t, w = q.shape
    seq, mlen = t // nb, k.shape[0] // nb
    tm = min(tm, seq)
    nt = seq // tm
    return pl.pallas_call(
        functools.partial(_xattn_kernel, heads=heads, dh=w // heads),
        grid=(nb, nt),
        in_specs=[pl.BlockSpec((tm, w), lambda b, i: (b * nt + i, 0)),
                  pl.BlockSpec((mlen, w), lambda b, i: (b, 0)),
                  pl.BlockSpec((mlen, w), lambda b, i: (b, 0))],
        out_specs=pl.BlockSpec((tm, w), lambda b, i: (b * nt + i, 0)),
        out_shape=jax.ShapeDtypeStruct((t, w), BF16),
        compiler_params=_params("parallel", "parallel"),
        name="cross_attention",
    )(q, k, v)


def kernel(x, mem, ffn1_norm, ffn1_w_gate, ffn1_w_up, ffn1_w_down, mix_norm, w_in, ssm_A_re, ssm_A_im, ssm_log_step, ssm_B_re, ssm_B_im, ssm_C_re, ssm_C_im, ssm_D, ssm_w_glu, ssm_out_norm, fox_b_f, fox_q_norm, fox_k_norm, fox_out_norm, w_out, xattn_norm, mem_norm, xattn_wq, xattn_wk, xattn_wv, xattn_q_norm, xattn_k_norm, xattn_wo, ffn2_norm, ffn2_w_gate, ffn2_w_up, ffn2_w_down, final_norm):
    nb, seq, d = x.shape
    t = nb * seq
    depth = ffn1_norm.shape[0]
    g, n_state = ssm_A_re.shape[1:]
    p = ssm_B_re.shape[-1]
    w_ssm = g * p
    fox_h, fox_dh = fox_b_f.shape[-1], fox_q_norm.shape[-1]
    w_fox = fox_h * fox_dh
    o_q, o_k, o_v, o_f = w_ssm, w_ssm + w_fox, w_ssm + 2 * w_fox, w_ssm + 3 * w_fox
    x_dh = xattn_q_norm.shape[-1]
    x_w = xattn_wq.shape[-1]
    x_heads = x_w // x_dh
    mlen = mem.shape[1]
    tn_in = min(1024, w_ssm, w_fox)
    assert w_ssm % w_fox == 0

    h = x.reshape(t, d)
    mem2 = mem.reshape(nb * mlen, d)

    for l in range(depth):
        if l == 0:
            h = ffn(h, ffn1_norm[l], cast_bf16(ffn1_w_gate, l), cast_bf16(ffn1_w_up, l),
                    cast_bf16(ffn1_w_down, l, scale=0.5))
        else:
            h = ffn(h, ffn1_norm[l], cast_bf16_tiles(ffn1_w_gate, l, 256),
                    cast_bf16_tiles(ffn1_w_up, l, 256),
                    cast_bf16(ffn1_w_down, l, scale=0.5), x_buffers=2)

        hn = rmsnorm(h, mix_norm[l], BF16)
        w_in_b = cast_bf16(w_in, l)
        gain_in = jnp.concatenate([
            jnp.ones((w_ssm,), F32),
            jnp.tile(fox_q_norm[l].astype(F32) * (fox_dh ** -0.5 * LOG2E), fox_h),
            jnp.tile(fox_k_norm[l].astype(F32), fox_h),
            jnp.ones((w_fox,), F32)])
        z = matmul(hn, w_in_b, BF16, tm=1024, tn=tn_in, n_cols=o_f, norm_gain=gain_in,
                   norm_group=fox_dh, norm_cols=(o_q, o_v))
        w_f = jnp.pad(w_in_b[:, o_f:], ((0, 0), (0, LANES - fox_h)))
        f_logit = matmul(hn, w_f, F32, tm=1024, tn=LANES)
        z3 = z.reshape(nb, seq, o_f)

        abar_r, abar_i, bbar_r, bbar_i = s5_discretise(
            ssm_A_re[l], ssm_A_im[l], ssm_log_step[l], ssm_B_re[l], ssm_B_im[l])
        y_ssm = s5_core(z3, abar_r, abar_i, bbar_r, bbar_i,
                        ssm_C_re[l].astype(F32), ssm_C_im[l].astype(F32), ssm_D[l])
        y = glu_norm(y_ssm.reshape(t, w_ssm), cast_bf16(ssm_w_glu, l), ssm_out_norm[l],
                     out_cols=w_ssm + w_fox)

        b_f_row = jnp.pad(fox_b_f[l].astype(F32), (0, LANES - fox_h)).reshape(1, LANES)
        c = forget_cumsum(f_logit.reshape(nb, seq, LANES), b_f_row)
        y_fox = fox_attention(z3, c, q_off=o_q, k_off=o_k, v_off=o_v, heads=fox_h, dh=fox_dh)
        y = rmsnorm(y_fox.reshape(t, w_fox), fox_out_norm[l], BF16, into=y,
                    col_block=w_ssm // w_fox)

        h = matmul(y, cast_bf16(w_out, l), F32, tm=1024, tn=512 if l == 0 else 1024, res=h)

        hn = rmsnorm(h, xattn_norm[l], BF16)
        mn = rmsnorm(mem2, mem_norm[l], BF16)
        q = matmul(hn, cast_bf16(xattn_wq, l), BF16, tm=1024, tn=x_dh,
                   norm_gain=jnp.tile(xattn_q_norm[l].astype(F32) * (x_dh ** -0.5), x_heads),
                   norm_group=x_dh)
        k = matmul(mn, cast_bf16(xattn_wk, l), BF16, tm=1024, tn=x_dh,
                   norm_gain=jnp.tile(xattn_k_norm[l].astype(F32), x_heads), norm_group=x_dh)
        v = matmul(mn, cast_bf16(xattn_wv, l), BF16, tm=1024, tn=x_dh)
        o = cross_attention(q, k, v, nb=nb, heads=x_heads)
        h = matmul(o, cast_bf16(xattn_wo, l), F32, tm=1024, tn=512 if l == 0 else 1024, res=h)

        if l == 0:
            h = ffn(h, ffn2_norm[l], cast_bf16_tiles(ffn2_w_gate, l, 256),
                    cast_bf16_tiles(ffn2_w_up, l, 256),
                    cast_bf16(ffn2_w_down, l, scale=0.5), final_gain=final_norm[l])
        else:
            h = ffn(h, ffn2_norm[l], cast_bf16(ffn2_w_gate, l), cast_bf16(ffn2_w_up, l),
                    cast_bf16(ffn2_w_down, l, scale=0.5), final_gain=final_norm[l], x_buffers=2)

    return h.reshape(nb, seq, d)
```

```python
import functools
import math

import jax
import jax.numpy as jnp
from jax import lax
from jax.experimental import pallas as pl
from jax.experimental.pallas import tpu as pltpu

F32 = jnp.float32
BF16 = jnp.bfloat16
EPS = 1e-6
NEG_BIG = -1e30

V7X_VMEM_LIMIT_BYTES = 58 * 1024 * 1024
LANES = 128
SUBLANES = 8


def _params(*sem):
    return pltpu.CompilerParams(dimension_semantics=sem,
                                vmem_limit_bytes=V7X_VMEM_LIMIT_BYTES)


def _rms(x, g):
    ms = jnp.mean(x * x, axis=-1, keepdims=True)
    return x * lax.rsqrt(ms + EPS) * g


def _cast_kernel(w_ref, o_ref, *, scale):
    w = w_ref[...]
    if scale != 1.0:
        w = w * scale
    o_ref[...] = w.astype(o_ref.dtype)


def cast_bf16(w_stack, layer, *, scale=1.0, tr=256):
    _, k, n = w_stack.shape
    tr = min(tr, k)
    assert k % tr == 0
    return pl.pallas_call(
        functools.partial(_cast_kernel, scale=scale),
        grid=(k // tr,),
        in_specs=[pl.BlockSpec((None, tr, n), lambda i: (layer, i, 0))],
        out_specs=pl.BlockSpec((tr, n), lambda i: (i, 0)),
        out_shape=jax.ShapeDtypeStruct((k, n), BF16),
        compiler_params=_params("parallel"),
        name="cast_bf16",
    )(w_stack)

def _rmsnorm_kernel(x_ref, g_ref, *rest):
    o_ref = rest[-1]
    o_ref[...] = _rms(x_ref[...].astype(F32), g_ref[...]).astype(o_ref.dtype)


def rmsnorm(x, g, out_dtype, tm=256, into=None, col_block=0):
    m, k = x.shape
    tm = min(tm, m)
    in_specs = [pl.BlockSpec((tm, k), lambda i: (i, 0)),
                pl.BlockSpec((1, k), lambda i: (0, 0))]
    args = [x, g.reshape(1, k).astype(F32)]
    aliases = {}
    out_shape = jax.ShapeDtypeStruct((m, k), out_dtype)
    if into is not None:
        assert into.shape[0] == m and into.shape[1] % k == 0 and into.dtype == out_dtype
        in_specs.append(pl.BlockSpec(memory_space=pl.ANY))
        args.append(into)
        aliases = {2: 0}
        out_shape = jax.ShapeDtypeStruct(into.shape, into.dtype)
    return pl.pallas_call(
        _rmsnorm_kernel,
        grid=(m // tm,),
        in_specs=in_specs,
        out_specs=pl.BlockSpec((tm, k), lambda i: (i, col_block)),
        out_shape=out_shape,
        input_output_aliases=aliases,
        compiler_params=_params("parallel"),
        name="rmsnorm",
    )(*args)


def _mm_kernel(*refs, has_res, norm_group, norm_lo, norm_hi):
    x_ref, w_ref = refs[0], refs[1]
    pos = 2
    res_ref = g_ref = None
    if has_res:
        res_ref = refs[pos]
        pos += 1
    if norm_group:
        g_ref = refs[pos]
        pos += 1
    o_ref = refs[pos]

    acc = jnp.dot(x_ref[...], w_ref[...], preferred_element_type=F32)
    if has_res:
        acc = acc + res_ref[...]
    if not norm_group:
        o_ref[...] = acc.astype(o_ref.dtype)
        return

    j = pl.program_id(1)
    in_range = jnp.logical_and(j >= norm_lo, j < norm_hi)

    @pl.when(in_range)
    def _():
        for c in range(acc.shape[1] // norm_group):
            sl = slice(c * norm_group, (c + 1) * norm_group)
            o_ref[:, sl] = _rms(acc[:, sl], g_ref[:, sl]).astype(o_ref.dtype)

    @pl.when(jnp.logical_not(in_range))
    def _():
        o_ref[...] = acc.astype(o_ref.dtype)


def matmul(x, w, out_dtype, *, tm=512, tn=512, n_cols=None, res=None,
           norm_gain=None, norm_group=0, norm_cols=None):
    m, k = x.shape
    n = w.shape[1] if n_cols is None else n_cols
    tm, tn = min(tm, m), min(tn, n)
    assert m % tm == 0 and n % tn == 0
    in_specs = [pl.BlockSpec((tm, k), lambda i, j: (i, 0)),
                pl.BlockSpec((k, tn), lambda i, j: (0, j))]
    args = [x, w]
    if res is not None:
        in_specs.append(pl.BlockSpec((tm, tn), lambda i, j: (i, j)))
        args.append(res)
    norm_lo = norm_hi = 0
    if norm_group:
        lo, hi = (0, n) if norm_cols is None else norm_cols
        assert tn % norm_group == 0 and lo % tn == 0 and hi % tn == 0
        norm_lo, norm_hi = lo // tn, hi // tn
        in_specs.append(pl.BlockSpec((1, tn), lambda i, j: (0, j)))
        args.append(norm_gain.reshape(1, n).astype(F32))
    return pl.pallas_call(
        functools.partial(_mm_kernel, has_res=res is not None, norm_group=norm_group,
                          norm_lo=norm_lo, norm_hi=norm_hi),
        grid=(m // tm, n // tn),
        in_specs=in_specs,
        out_specs=pl.BlockSpec((tm, tn), lambda i, j: (i, j)),
        out_shape=jax.ShapeDtypeStruct((m, n), out_dtype),
        compiler_params=_params("parallel", "parallel"),
        name="matmul",
    )(*args)


def _ffn_kernel(*refs, final_norm):
    if final_norm:
        x_ref, g_ref, wg_ref, wu_ref, wd_ref, fg_ref, o_ref, n_scr = refs
    else:
        x_ref, g_ref, wg_ref, wu_ref, wd_ref, o_ref, n_scr = refs
        fg_ref = None
    j = pl.program_id(1)

    @pl.when(j == 0)
    def _():
        x = x_ref[...]
        n_scr[...] = _rms(x, g_ref[...]).astype(BF16)
        o_ref[...] = x

    n = n_scr[...]
    gate = jnp.dot(n, wg_ref[...], preferred_element_type=F32)
    up = jnp.dot(n, wu_ref[...], preferred_element_type=F32)
    hid = (gate * jax.nn.sigmoid(gate) * up).astype(BF16)
    o_ref[...] += jnp.dot(hid, wd_ref[...], preferred_element_type=F32)

    if final_norm:
        @pl.when(j == pl.num_programs(1) - 1)
        def _():
            o_ref[...] = _rms(o_ref[...], fg_ref[...])


def ffn(x, g, wg, wu, wd_half, final_gain=None, *, tm=512, tf=256):
    m, d = x.shape
    dff = wg.shape[1]
    tm, tf = min(tm, m), min(tf, dff)
    assert m % tm == 0 and dff % tf == 0
    in_specs = [pl.BlockSpec((tm, d), lambda i, j: (i, 0)),
                pl.BlockSpec((1, d), lambda i, j: (0, 0)),
                pl.BlockSpec((d, tf), lambda i, j: (0, j)),
                pl.BlockSpec((d, tf), lambda i, j: (0, j)),
                pl.BlockSpec((tf, d), lambda i, j: (j, 0))]
    args = [x, g.reshape(1, d).astype(F32), wg, wu, wd_half]
    if final_gain is not None:
        in_specs.append(pl.BlockSpec((1, d), lambda i, j: (0, 0)))
        args.append(final_gain.reshape(1, d).astype(F32))
    return pl.pallas_call(
        functools.partial(_ffn_kernel, final_norm=final_gain is not None),
        grid=(m // tm, dff // tf),
        in_specs=in_specs,
        out_specs=pl.BlockSpec((tm, d), lambda i, j: (i, 0)),
        out_shape=jax.ShapeDtypeStruct((m, d), F32),
        scratch_shapes=[pltpu.VMEM((tm, d), BF16)],
        compiler_params=_params("parallel", "arbitrary"),
        name="ffn",
    )(*args)


def _s5_disc_kernel(are_ref, aim_ref, ls_ref, bre_ref, bim_ref,
                    abr_ref, abi_ref, bbr_ref, bbi_ref):
    ar, ai = are_ref[...], aim_ref[...]
    dt = jnp.exp(ls_ref[...])
    mag = jnp.exp(ar * dt)
    abar_r = mag * jnp.cos(ai * dt)
    abar_i = mag * jnp.sin(ai * dt)
    den = ar * ar + ai * ai
    pr, pi_ = abar_r - 1.0, abar_i
    coef_r = (pr * ar + pi_ * ai) / den
    coef_i = (pi_ * ar - pr * ai) / den
    br, bi = bre_ref[...], bim_ref[...]
    abr_ref[...] = abar_r
    abi_ref[...] = abar_i
    bbr_ref[...] = coef_r * br - coef_i * bi
    bbi_ref[...] = coef_r * bi + coef_i * br


def s5_discretise(a_re, a_im, log_step, b_re, b_im):
    g, n = a_re.shape
    p = b_re.shape[-1]
    col = jax.ShapeDtypeStruct((g, n, 1), F32)
    mat = jax.ShapeDtypeStruct((g, n, p), F32)
    ls = jnp.broadcast_to(log_step.reshape(g, 1, 1), (g, n, 1))
    abr, abi, bbr, bbi = pl.pallas_call(
        _s5_disc_kernel,
        out_shape=(col, col, mat, mat),
        compiler_params=pltpu.CompilerParams(vmem_limit_bytes=V7X_VMEM_LIMIT_BYTES),
        name="s5_discretise",
    )(a_re.reshape(g, n, 1), a_im.reshape(g, n, 1), ls, b_re, b_im)
    return abr.reshape(g, n), abi.reshape(g, n), bbr, bbi


GROUPS_PER_HALF = 8


def _gelu_tanh(x):
    c = math.sqrt(2.0 / math.pi)
    return 0.5 * x * (1.0 + jnp.tanh(c * (x + 0.044715 * (x * x * x))))


def _s5_kernel(u_ref, rb_ref, rc_ref, ar_ref, ai_ref, d_ref, y_ref,
               lhs_scr, sr_scr, si_scr, o2_scr, xr_scr, xi_scr, *, nb, hs, tt, sw, hw):
    @pl.when(pl.program_id(1) == 0)
    def _():
        xr_scr[...] = jnp.zeros_like(xr_scr)
        xi_scr[...] = jnp.zeros_like(xi_scr)

    zeros = jnp.zeros((tt, hw), F32)
    for b in range(nb):
        ub = u_ref[b].astype(F32)
        for hh in range(hs):
            rows = pl.ds(hh * nb + b, tt, stride=SUBLANES)
            for c in range(hs):
                lhs_scr[c, rows, :] = ub[:, c * hw:(c + 1) * hw] if c == hh else zeros

    lhs = jnp.concatenate([lhs_scr[c] for c in range(hs)], axis=-1).astype(BF16)
    hm = lhs.shape[0] // 2
    for r0 in (0, hm):
        sr_scr[r0:r0 + hm, :] = jnp.dot(lhs[r0:r0 + hm], rb_ref[0, :, :sw],
                                        preferred_element_type=F32)
        si_scr[r0:r0 + hm, :] = jnp.dot(lhs[r0:r0 + hm], rb_ref[0, :, sw:],
                                        preferred_element_type=F32)

    ar, ai = ar_ref[0], ai_ref[0]

    def step(t, carry):
        xr, xi = carry
        rows = pl.ds(pl.multiple_of(t * SUBLANES, SUBLANES), SUBLANES)
        nxr = ar * xr - ai * xi + sr_scr[rows, :]
        nxi = ar * xi + ai * xr + si_scr[rows, :]
        sr_scr[rows, :] = nxr
        si_scr[rows, :] = nxi
        return nxr, nxi

    xr, xi = lax.fori_loop(0, tt, step, (xr_scr[...], xi_scr[...]), unroll=8)
    xr_scr[...] = xr
    xi_scr[...] = xi

    for r0 in (0, hm):
        o2 = (jnp.dot(sr_scr[r0:r0 + hm, :].astype(BF16), rc_ref[0, :sw, :],
                      preferred_element_type=F32)
              + jnp.dot(si_scr[r0:r0 + hm, :].astype(BF16), rc_ref[0, sw:, :],
                        preferred_element_type=F32))
        for c in range(hs):
            o2_scr[c, r0:r0 + hm, :] = o2[:, c * hw:(c + 1) * hw]

    for b in range(nb):
        parts = [o2_scr[hh, pl.ds(hh * nb + b, tt, stride=SUBLANES), :] for hh in range(hs)]
        yb = jnp.concatenate(parts, axis=-1) + d_ref[...] * u_ref[b].astype(F32)
        y_ref[b] = _gelu_tanh(yb).astype(y_ref.dtype)


def s5_core(z3, abar_r, abar_i, bbar_r, bbar_i, c_re, c_im, d_skip, *, tt=256):
    nb, seq, _ = z3.shape
    g, n, p = bbar_r.shape
    assert SUBLANES % nb == 0
    hs = SUBLANES // nb
    gph = GROUPS_PER_HALF
    assert g % (hs * gph) == 0
    nblk = g // (hs * gph)
    hw, sw = gph * p, gph * n
    cw = hs * hw
    assert hw == LANES and sw % LANES == 0
    tt = min(tt, seq)
    assert seq % tt == 0

    eye = jnp.eye(gph, dtype=F32)

    def in_proj(bb):
        bb = bb.reshape(nblk, hs, gph, n, p)
        return jnp.einsum("jhqnp,qr->jhqprn", bb, eye).reshape(nblk, cw, sw)

    def out_proj(cc):
        cc = cc.reshape(nblk, hs, gph, p, n)
        return jnp.einsum("jhrpn,qr->jqnhrp", cc, eye).reshape(nblk, sw, cw)

    rb = jnp.concatenate([in_proj(bbar_r), in_proj(bbar_i)], axis=-1).astype(BF16)
    rc = jnp.concatenate([out_proj(c_re), out_proj(-c_im)], axis=1).astype(BF16)

    def rows(a):
        a = a.reshape(nblk, hs, 1, sw)
        return jnp.broadcast_to(a, (nblk, hs, nb, sw)).reshape(nblk, hs * nb, sw)

    return pl.pallas_call(
        functools.partial(_s5_kernel, nb=nb, hs=hs, tt=tt, sw=sw, hw=hw),
        grid=(nblk, seq // tt),
        in_specs=[pl.BlockSpec((nb, tt, cw), lambda j, t: (0, t, j)),
                  pl.BlockSpec((1, cw, 2 * sw), lambda j, t: (j, 0, 0)),
                  pl.BlockSpec((1, 2 * sw, cw), lambda j, t: (j, 0, 0)),
                  pl.BlockSpec((1, SUBLANES, sw), lambda j, t: (j, 0, 0)),
                  pl.BlockSpec((1, SUBLANES, sw), lambda j, t: (j, 0, 0)),
                  pl.BlockSpec((1, cw), lambda j, t: (0, j))],
        out_specs=pl.BlockSpec((nb, tt, cw), lambda j, t: (0, t, j)),
        out_shape=jax.ShapeDtypeStruct((nb, seq, g * p), BF16),
        scratch_shapes=[pltpu.VMEM((hs, SUBLANES * tt, hw), F32),
                        pltpu.VMEM((SUBLANES * tt, sw), F32),
                        pltpu.VMEM((SUBLANES * tt, sw), F32),
                        pltpu.VMEM((hs, SUBLANES * tt, hw), F32),
                        pltpu.VMEM((SUBLANES, sw), F32),
                        pltpu.VMEM((SUBLANES, sw), F32)],
        compiler_params=_params("parallel", "arbitrary"),
        name="s5_core",
    )(z3, rb, rc, rows(abar_r), rows(abar_i), d_skip.reshape(1, g * p).astype(F32))


def _glu_kernel(y_ref, w_ref, g_ref, o_ref):
    y = y_ref[...]
    v = jnp.dot(y, w_ref[...], preferred_element_type=F32)
    o = y.astype(F32) * jax.nn.sigmoid(v)
    o_ref[...] = _rms(o, g_ref[...]).astype(o_ref.dtype)


def glu_norm(y, w, g, *, out_cols, tm=512):
    m, k = y.shape
    tm = min(tm, m)
    return pl.pallas_call(
        _glu_kernel,
        grid=(m // tm,),
        in_specs=[pl.BlockSpec((tm, k), lambda i: (i, 0)),
                  pl.BlockSpec((k, k), lambda i: (0, 0)),
                  pl.BlockSpec((1, k), lambda i: (0, 0))],
        out_specs=pl.BlockSpec((tm, k), lambda i: (i, 0)),
        out_shape=jax.ShapeDtypeStruct((m, out_cols), BF16),
        compiler_params=_params("parallel"),
        name="glu_norm",
    )(y, w, g.reshape(1, k).astype(F32))


LOG2E = math.log2(math.e)
def _cumsum_kernel(f_ref, bf_ref, c_ref, *, chunk):
    seq = f_ref.shape[1]
    row = lax.broadcasted_iota(jnp.int32, (chunk, chunk), 0)
    col = lax.broadcasted_iota(jnp.int32, (chunk, chunk), 1)
    tri = (col <= row).astype(F32)
    carry = jnp.zeros((1, f_ref.shape[2]), F32)
    for ci in range(seq // chunk):
        sl = slice(ci * chunk, (ci + 1) * chunk)
        x = f_ref[0, sl, :] + bf_ref[...]
        log_f = (jnp.minimum(x, 0.0) - jnp.log1p(jnp.exp(-jnp.abs(x)))) * LOG2E
        cs = jnp.dot(tri, log_f, precision=lax.Precision.HIGHEST,
                     preferred_element_type=F32) + carry
        c_ref[0, sl, :] = cs
        carry = cs[chunk - 1:chunk, :]


def forget_cumsum(f3, b_f_row, *, chunk=256):
    nb, seq, w = f3.shape
    chunk = min(chunk, seq)
    return pl.pallas_call(
        functools.partial(_cumsum_kernel, chunk=chunk),
        grid=(nb,),
        in_specs=[pl.BlockSpec((1, seq, w), lambda b: (b, 0, 0)),
                  pl.BlockSpec((1, w), lambda b: (0, 0))],
        out_specs=pl.BlockSpec((1, seq, w), lambda b: (b, 0, 0)),
        out_shape=jax.ShapeDtypeStruct((nb, seq, w), F32),
        compiler_params=_params("parallel"),
        name="forget_cumsum",
    )(f3, b_f_row)


FOX_HEADS_PER_STEP = 2


def _fox_kernel(q_ref, k_ref, v_ref, cq_ref, ck_ref, o_ref, *, tq, big, nh, dh):
    i = pl.program_id(2)
    per_big = big // tq

    def scores(hd, start, width, diagonal):
        cols = slice(hd * dh, (hd + 1) * dh)
        rows = pl.ds(start, width)
        s = lax.dot_general(q_ref[0, :, cols], k_ref[0, rows, cols], (((1,), (1,)), ((), ())),
                            preferred_element_type=F32)
        s = s + (cq_ref[0, hd] - ck_ref[0, hd, :, rows])
        if diagonal:
            r = lax.broadcasted_iota(jnp.int32, s.shape, 0)
            c = lax.broadcasted_iota(jnp.int32, s.shape, 1)
            s = jnp.where(c <= r, s, NEG_BIG)
        return s

    def update(hd, s, start, width, carry):
        m, l, acc = carry
        v = v_ref[0, pl.ds(start, width), hd * dh:(hd + 1) * dh]
        m_new = jnp.maximum(m, jnp.max(s, axis=-1, keepdims=True))
        alpha = jnp.exp2(m - m_new)
        p = jnp.exp2(s - m_new)
        l = alpha * l + jnp.sum(p, axis=-1, keepdims=True)
        acc = alpha * acc + jnp.dot(p.astype(v.dtype), v, preferred_element_type=F32)
        return m_new, l, acc

    def block(start, width, carries, diagonal):
        ss = [scores(hd, start, width, diagonal) for hd in range(nh)]
        return tuple(update(hd, ss[hd], start, width, carries[hd]) for hd in range(nh))

    init = tuple((jnp.full((tq, 1), NEG_BIG, F32), jnp.zeros((tq, 1), F32),
                  jnp.zeros((tq, dh), F32)) for _ in range(nh))
    n_big = i // per_big
    carries = lax.fori_loop(
        0, n_big, lambda j, c: block(pl.multiple_of(j * big, big), big, c, False), init)
    carries = lax.fori_loop(
        n_big * per_big, i, lambda j, c: block(pl.multiple_of(j * tq, tq), tq, c, False), carries)
    carries = block(pl.multiple_of(i * tq, tq), tq, carries, True)
    for hd in range(nh):
        _, l, acc = carries[hd]
        o_ref[0, :, hd * dh:(hd + 1) * dh] = (acc / l).astype(o_ref.dtype)


def fox_attention(z3, c, *, q_off, k_off, v_off, heads, dh, tq=512, big=1024):
    nb, seq, _ = z3.shape
    tq = min(tq, seq)
    big = min(big, seq)
    nq = seq // tq
    nh = FOX_HEADS_PER_STEP if heads % FOX_HEADS_PER_STEP == 0 else 1
    w = nh * dh
    assert seq % tq == 0 and big % tq == 0
    assert q_off % w == 0 and k_off % w == 0 and v_off % w == 0
    ch = jnp.transpose(c[:, :, :heads], (0, 2, 1))
    cq = ch.reshape(nb, heads, seq, 1)
    ck = ch.reshape(nb, heads, 1, seq)
    qb, kb, vb = q_off // w, k_off // w, v_off // w
    return pl.pallas_call(
        functools.partial(_fox_kernel, tq=tq, big=big, nh=nh, dh=dh),
        grid=(nb, heads // nh, nq),
        in_specs=[pl.BlockSpec((1, tq, w), lambda b, h, i: (b, i, qb + h)),
                  pl.BlockSpec((1, seq, w), lambda b, h, i: (b, 0, kb + h)),
                  pl.BlockSpec((1, seq, w), lambda b, h, i: (b, 0, vb + h)),
                  pl.BlockSpec((1, nh, tq, 1), lambda b, h, i: (b, h, i, 0)),
                  pl.BlockSpec((1, nh, 1, seq), lambda b, h, i: (b, h, 0, 0))],
        out_specs=pl.BlockSpec((1, tq, w), lambda b, h, i: (b, i, h)),
        out_shape=jax.ShapeDtypeStruct((nb, seq, heads * dh), F32),
        compiler_params=_params("parallel", "parallel", "parallel"),
        name="fox_attention",
    )(z3, z3, z3, cq, ck)


def _xattn_kernel(q_ref, k_ref, v_ref, o_ref, *, heads, dh):
    scores = [lax.dot_general(q_ref[:, hd * dh:(hd + 1) * dh], k_ref[:, hd * dh:(hd + 1) * dh],
                              (((1,), (1,)), ((), ())), preferred_element_type=F32)
              for hd in range(heads)]
    for hd in range(heads):
        sl = slice(hd * dh, (hd + 1) * dh)
        s = scores[hd]
        s = s - jnp.max(s, axis=-1, keepdims=True)
        p = jnp.exp(s)
        p = p / jnp.sum(p, axis=-1, keepdims=True)
        o_ref[:, sl] = jnp.dot(p.astype(BF16), v_ref[:, sl],
                               preferred_element_type=F32).astype(o_ref.dtype)


def cross_attention(q, k, v, *, nb, heads, tm=512):
    t, w = q.shape
    seq, mlen = t // nb, k.shape[0] // nb
    tm = min(tm, seq)
    nt = seq // tm
    return pl.pallas_call(
        functools.partial(_xattn_kernel, heads=heads, dh=w // heads),
        grid=(nb, nt),
        in_specs=[pl.BlockSpec((tm, w), lambda b, i: (b * nt + i, 0)),
                  pl.BlockSpec((mlen, w), lambda b, i: (b, 0)),
                  pl.BlockSpec((mlen, w), lambda b, i: (b, 0))],
        out_specs=pl.BlockSpec((tm, w), lambda b, i: (b * nt + i, 0)),
        out_shape=jax.ShapeDtypeStruct((t, w), BF16),
        compiler_params=_params("parallel", "parallel"),
        name="cross_attention",
    )(q, k, v)


def kernel(x, mem, ffn1_norm, ffn1_w_gate, ffn1_w_up, ffn1_w_down, mix_norm, w_in, ssm_A_re, ssm_A_im, ssm_log_step, ssm_B_re, ssm_B_im, ssm_C_re, ssm_C_im, ssm_D, ssm_w_glu, ssm_out_norm, fox_b_f, fox_q_norm, fox_k_norm, fox_out_norm, w_out, xattn_norm, mem_norm, xattn_wq, xattn_wk, xattn_wv, xattn_q_norm, xattn_k_norm, xattn_wo, ffn2_norm, ffn2_w_gate, ffn2_w_up, ffn2_w_down, final_norm):
    nb, seq, d = x.shape
    t = nb * seq
    depth = ffn1_norm.shape[0]
    g, n_state = ssm_A_re.shape[1:]
    p = ssm_B_re.shape[-1]
    w_ssm = g * p
    fox_h, fox_dh = fox_b_f.shape[-1], fox_q_norm.shape[-1]
    w_fox = fox_h * fox_dh
    o_q, o_k, o_v, o_f = w_ssm, w_ssm + w_fox, w_ssm + 2 * w_fox, w_ssm + 3 * w_fox
    x_dh = xattn_q_norm.shape[-1]
    x_w = xattn_wq.shape[-1]
    x_heads = x_w // x_dh
    mlen = mem.shape[1]
    tn_in = min(1024, w_ssm, w_fox)
    assert w_ssm % w_fox == 0

    h = x.reshape(t, d)
    mem2 = mem.reshape(nb * mlen, d)

    for l in range(depth):
        h = ffn(h, ffn1_norm[l], cast_bf16(ffn1_w_gate, l), cast_bf16(ffn1_w_up, l),
                cast_bf16(ffn1_w_down, l, scale=0.5))

        hn = rmsnorm(h, mix_norm[l], BF16)
        w_in_b = cast_bf16(w_in, l)
        gain_in = jnp.concatenate([
            jnp.ones((w_ssm,), F32),
            jnp.tile(fox_q_norm[l].astype(F32) * (fox_dh ** -0.5 * LOG2E), fox_h),
            jnp.tile(fox_k_norm[l].astype(F32), fox_h),
            jnp.ones((w_fox,), F32)])
        z = matmul(hn, w_in_b, BF16, tm=1024, tn=tn_in, n_cols=o_f, norm_gain=gain_in,
                   norm_group=fox_dh, norm_cols=(o_q, o_v))
        w_f = jnp.pad(w_in_b[:, o_f:], ((0, 0), (0, LANES - fox_h)))
        f_logit = matmul(hn, w_f, F32, tm=1024, tn=LANES)
        z3 = z.reshape(nb, seq, o_f)

        abar_r, abar_i, bbar_r, bbar_i = s5_discretise(
            ssm_A_re[l], ssm_A_im[l], ssm_log_step[l], ssm_B_re[l], ssm_B_im[l])
        y_ssm = s5_core(z3, abar_r, abar_i, bbar_r, bbar_i,
                        ssm_C_re[l].astype(F32), ssm_C_im[l].astype(F32), ssm_D[l])
        y = glu_norm(y_ssm.reshape(t, w_ssm), cast_bf16(ssm_w_glu, l), ssm_out_norm[l],
                     out_cols=w_ssm + w_fox)

        b_f_row = jnp.pad(fox_b_f[l].astype(F32), (0, LANES - fox_h)).reshape(1, LANES)
        c = forget_cumsum(f_logit.reshape(nb, seq, LANES), b_f_row)
        y_fox = fox_attention(z3, c, q_off=o_q, k_off=o_k, v_off=o_v, heads=fox_h, dh=fox_dh)
        y = rmsnorm(y_fox.reshape(t, w_fox), fox_out_norm[l], BF16, into=y,
                    col_block=w_ssm // w_fox)

        h = matmul(y, cast_bf16(w_out, l), F32, tm=1024, tn=1024, res=h)

        hn = rmsnorm(h, xattn_norm[l], BF16)
        mn = rmsnorm(mem2, mem_norm[l], BF16)
        q = matmul(hn, cast_bf16(xattn_wq, l), BF16, tm=1024, tn=x_dh,
                   norm_gain=jnp.tile(xattn_q_norm[l].astype(F32) * (x_dh ** -0.5), x_heads),
                   norm_group=x_dh)
        k = matmul(mn, cast_bf16(xattn_wk, l), BF16, tm=1024, tn=x_dh,
                   norm_gain=jnp.tile(xattn_k_norm[l].astype(F32), x_heads), norm_group=x_dh)
        v = matmul(mn, cast_bf16(xattn_wv, l), BF16, tm=1024, tn=x_dh)
        o = cross_attention(q, k, v, nb=nb, heads=x_heads)
        h = matmul(o, cast_bf16(xattn_wo, l), F32, tm=1024, tn=1024, res=h)

        h = ffn(h, ffn2_norm[l], cast_bf16(ffn2_w_gate, l), cast_bf16(ffn2_w_up, l),
                cast_bf16(ffn2_w_down, l, scale=0.5), final_gain=final_norm[l])

    return h.reshape(nb, seq, d)
```

```python
import functools
import math

import jax
import jax.numpy as jnp
from jax import lax
from jax.experimental import pallas as pl
from jax.experimental.pallas import tpu as pltpu

F32 = jnp.float32
BF16 = jnp.bfloat16
EPS = 1e-6
NEG_BIG = -1e30

V7X_VMEM_LIMIT_BYTES = 58 * 1024 * 1024
LANES = 128
SUBLANES = 8


def _params(*sem):
    return pltpu.CompilerParams(dimension_semantics=sem,
                                vmem_limit_bytes=V7X_VMEM_LIMIT_BYTES)


def _rms(x, g):
    ms = jnp.mean(x * x, axis=-1, keepdims=True)
    return x * lax.rsqrt(ms + EPS) * g


def _cast_kernel(w_ref, o_ref, *, scale):
    w = w_ref[...]
    if scale != 1.0:
        w = w * scale
    o_ref[...] = w.astype(o_ref.dtype)


def cast_bf16(w_stack, layer, *, scale=1.0, tr=256):
    _, k, n = w_stack.shape
    tr = min(tr, k)
    assert k % tr == 0
    return pl.pallas_call(
        functools.partial(_cast_kernel, scale=scale),
        grid=(k // tr,),
        in_specs=[pl.BlockSpec((None, tr, n), lambda i: (layer, i, 0))],
        out_specs=pl.BlockSpec((tr, n), lambda i: (i, 0)),
        out_shape=jax.ShapeDtypeStruct((k, n), BF16),
        compiler_params=_params("parallel"),
        name="cast_bf16",
    )(w_stack)

BF16_SUBLANE_TILE = 16


def _side_block(k, n, steps):
    if k % steps == 0 and (k // steps) % BF16_SUBLANE_TILE == 0:
        return (k // steps, n), lambda s: (s, 0)
    for s1 in (16, 8, 4, 2, 1):
        s2 = steps // s1
        if (steps % s1 == 0 and k % s1 == 0 and (k // s1) % BF16_SUBLANE_TILE == 0
                and n % s2 == 0 and (n // s2) % LANES == 0):
            return (k // s1, n // s2), lambda s: (s // s2, s % s2)
    raise ValueError(f"cannot tile a ({k}, {n}) weight in {steps} blocks")


def _pallas_with_side_casts(body, *, grid, in_specs, out_specs, out_shape, args, side_casts=(),
                            **kwargs):
    out_specs, out_shape = list(out_specs), list(out_shape)
    n_in, n_out, n_side = len(in_specs), len(out_specs), len(side_casts)
    steps = math.prod(grid)
    strides = [math.prod(grid[a + 1:]) for a in range(len(grid))]
    scales = []
    for w_stack, layer, scale in side_casts:
        _, k, n = w_stack.shape
        blk, to_index = _side_block(k, n, steps)

        def index(*g, _to_index=to_index):
            return _to_index(sum(gi * st for gi, st in zip(g, strides)))

        in_specs = in_specs + [pl.BlockSpec((None,) + blk,
                                            lambda *g, _l=layer, _ix=index: (_l,) + _ix(*g))]
        out_specs.append(pl.BlockSpec(blk, index))
        out_shape.append(jax.ShapeDtypeStruct((k, n), BF16))
        args = list(args) + [w_stack]
        scales.append(scale)

    def kern(*refs):
        side_in = refs[n_in:n_in + n_side]
        outs = refs[n_in + n_side:n_in + n_side + n_out]
        side_out = refs[n_in + n_side + n_out:n_in + 2 * n_side + n_out]
        scratch = refs[n_in + 2 * n_side + n_out:]
        for si, so, scale in zip(side_in, side_out, scales):
            w = si[...]
            so[...] = (w if scale == 1.0 else w * scale).astype(so.dtype)
        body(*refs[:n_in], *outs, *scratch)

    res = pl.pallas_call(kern, grid=grid, in_specs=in_specs, out_specs=out_specs,
                         out_shape=out_shape, **kwargs)(*args)
    return tuple(res[:n_out]), tuple(res[n_out:])


def _rmsnorm_kernel(x_ref, g_ref, *rest):
    o_ref = rest[-1]
    o_ref[...] = _rms(x_ref[...].astype(F32), g_ref[...]).astype(o_ref.dtype)


def rmsnorm(x, g, out_dtype, tm=256, into=None, col_block=0):
    m, k = x.shape
    tm = min(tm, m)
    in_specs = [pl.BlockSpec((tm, k), lambda i: (i, 0)),
                pl.BlockSpec((1, k), lambda i: (0, 0))]
    args = [x, g.reshape(1, k).astype(F32)]
    aliases = {}
    out_shape = jax.ShapeDtypeStruct((m, k), out_dtype)
    if into is not None:
        assert into.shape[0] == m and into.shape[1] % k == 0 and into.dtype == out_dtype
        in_specs.append(pl.BlockSpec(memory_space=pl.ANY))
        args.append(into)
        aliases = {2: 0}
        out_shape = jax.ShapeDtypeStruct(into.shape, into.dtype)
    return pl.pallas_call(
        _rmsnorm_kernel,
        grid=(m // tm,),
        in_specs=in_specs,
        out_specs=pl.BlockSpec((tm, k), lambda i: (i, col_block)),
        out_shape=out_shape,
        input_output_aliases=aliases,
        compiler_params=_params("parallel"),
        name="rmsnorm",
    )(*args)


def _mm_kernel(*refs, has_res, norm_group, norm_lo, norm_hi):
    x_ref, w_ref = refs[0], refs[1]
    pos = 2
    res_ref = g_ref = None
    if has_res:
        res_ref = refs[pos]
        pos += 1
    if norm_group:
        g_ref = refs[pos]
        pos += 1
    o_ref = refs[pos]

    acc = jnp.dot(x_ref[...], w_ref[...], preferred_element_type=F32)
    if has_res:
        acc = acc + res_ref[...]
    if not norm_group:
        o_ref[...] = acc.astype(o_ref.dtype)
        return

    j = pl.program_id(1)
    in_range = jnp.logical_and(j >= norm_lo, j < norm_hi)

    @pl.when(in_range)
    def _():
        for c in range(acc.shape[1] // norm_group):
            sl = slice(c * norm_group, (c + 1) * norm_group)
            o_ref[:, sl] = _rms(acc[:, sl], g_ref[:, sl]).astype(o_ref.dtype)

    @pl.when(jnp.logical_not(in_range))
    def _():
        o_ref[...] = acc.astype(o_ref.dtype)


def matmul(x, w, out_dtype, *, tm=512, tn=512, n_cols=None, res=None,
           norm_gain=None, norm_group=0, norm_cols=None, side_casts=None):
    m, k = x.shape
    n = w.shape[1] if n_cols is None else n_cols
    tm, tn = min(tm, m), min(tn, n)
    assert m % tm == 0 and n % tn == 0
    in_specs = [pl.BlockSpec((tm, k), lambda i, j: (i, 0)),
                pl.BlockSpec((k, tn), lambda i, j: (0, j))]
    args = [x, w]
    if res is not None:
        in_specs.append(pl.BlockSpec((tm, tn), lambda i, j: (i, j)))
        args.append(res)
    norm_lo = norm_hi = 0
    if norm_group:
        lo, hi = (0, n) if norm_cols is None else norm_cols
        assert tn % norm_group == 0 and lo % tn == 0 and hi % tn == 0
        norm_lo, norm_hi = lo // tn, hi // tn
        in_specs.append(pl.BlockSpec((1, tn), lambda i, j: (0, j)))
        args.append(norm_gain.reshape(1, n).astype(F32))
    (out,), side = _pallas_with_side_casts(
        functools.partial(_mm_kernel, has_res=res is not None, norm_group=norm_group,
                          norm_lo=norm_lo, norm_hi=norm_hi),
        grid=(m // tm, n // tn),
        in_specs=in_specs,
        out_specs=[pl.BlockSpec((tm, tn), lambda i, j: (i, j))],
        out_shape=[jax.ShapeDtypeStruct((m, n), out_dtype)],
        args=args,
        side_casts=side_casts or (),
        compiler_params=_params("parallel", "parallel"),
        name="matmul",
    )
    return out if side_casts is None else (out, *side)


def _ffn_kernel(*refs, final_norm):
    if final_norm:
        x_ref, g_ref, wg_ref, wu_ref, wd_ref, fg_ref, o_ref, n_scr = refs
    else:
        x_ref, g_ref, wg_ref, wu_ref, wd_ref, o_ref, n_scr = refs
        fg_ref = None
    j = pl.program_id(1)

    @pl.when(j == 0)
    def _():
        x = x_ref[...]
        n_scr[...] = _rms(x, g_ref[...]).astype(BF16)
        o_ref[...] = x

    n = n_scr[...]
    gate = jnp.dot(n, wg_ref[...], preferred_element_type=F32)
    up = jnp.dot(n, wu_ref[...], preferred_element_type=F32)
    hid = (gate * jax.nn.sigmoid(gate) * up).astype(BF16)
    o_ref[...] += jnp.dot(hid, wd_ref[...], preferred_element_type=F32)

    if final_norm:
        @pl.when(j == pl.num_programs(1) - 1)
        def _():
            o_ref[...] = _rms(o_ref[...], fg_ref[...])


def ffn(x, g, wg, wu, wd_half, final_gain=None, *, tm=512, tf=256):
    m, d = x.shape
    dff = wg.shape[1]
    tm, tf = min(tm, m), min(tf, dff)
    assert m % tm == 0 and dff % tf == 0
    in_specs = [pl.BlockSpec((tm, d), lambda i, j: (i, 0)),
                pl.BlockSpec((1, d), lambda i, j: (0, 0)),
                pl.BlockSpec((d, tf), lambda i, j: (0, j)),
                pl.BlockSpec((d, tf), lambda i, j: (0, j)),
                pl.BlockSpec((tf, d), lambda i, j: (j, 0))]
    args = [x, g.reshape(1, d).astype(F32), wg, wu, wd_half]
    if final_gain is not None:
        in_specs.append(pl.BlockSpec((1, d), lambda i, j: (0, 0)))
        args.append(final_gain.reshape(1, d).astype(F32))
    return pl.pallas_call(
        functools.partial(_ffn_kernel, final_norm=final_gain is not None),
        grid=(m // tm, dff // tf),
        in_specs=in_specs,
        out_specs=pl.BlockSpec((tm, d), lambda i, j: (i, 0)),
        out_shape=jax.ShapeDtypeStruct((m, d), F32),
        scratch_shapes=[pltpu.VMEM((tm, d), BF16)],
        compiler_params=_params("parallel", "arbitrary"),
        name="ffn",
    )(*args)


def _s5_disc_kernel(are_ref, aim_ref, ls_ref, bre_ref, bim_ref,
                    abr_ref, abi_ref, bbr_ref, bbi_ref):
    ar, ai = are_ref[...], aim_ref[...]
    dt = jnp.exp(ls_ref[...])
    mag = jnp.exp(ar * dt)
    abar_r = mag * jnp.cos(ai * dt)
    abar_i = mag * jnp.sin(ai * dt)
    den = ar * ar + ai * ai
    pr, pi_ = abar_r - 1.0, abar_i
    coef_r = (pr * ar + pi_ * ai) / den
    coef_i = (pi_ * ar - pr * ai) / den
    br, bi = bre_ref[...], bim_ref[...]
    abr_ref[...] = abar_r
    abi_ref[...] = abar_i
    bbr_ref[...] = coef_r * br - coef_i * bi
    bbi_ref[...] = coef_r * bi + coef_i * br


def s5_discretise(a_re, a_im, log_step, b_re, b_im):
    g, n = a_re.shape
    p = b_re.shape[-1]
    col = jax.ShapeDtypeStruct((g, n, 1), F32)
    mat = jax.ShapeDtypeStruct((g, n, p), F32)
    ls = jnp.broadcast_to(log_step.reshape(g, 1, 1), (g, n, 1))
    abr, abi, bbr, bbi = pl.pallas_call(
        _s5_disc_kernel,
        out_shape=(col, col, mat, mat),
        compiler_params=pltpu.CompilerParams(vmem_limit_bytes=V7X_VMEM_LIMIT_BYTES),
        name="s5_discretise",
    )(a_re.reshape(g, n, 1), a_im.reshape(g, n, 1), ls, b_re, b_im)
    return abr.reshape(g, n), abi.reshape(g, n), bbr, bbi


GROUPS_PER_HALF = 8


def _gelu_tanh(x):
    c = math.sqrt(2.0 / math.pi)
    return 0.5 * x * (1.0 + jnp.tanh(c * (x + 0.044715 * (x * x * x))))


def _s5_kernel(u_ref, rb_ref, rc_ref, ar_ref, ai_ref, d_ref, y_ref,
               lhs_scr, sr_scr, si_scr, o2_scr, xr_scr, xi_scr, *, nb, hs, tt, sw, hw):
    @pl.when(pl.program_id(1) == 0)
    def _():
        xr_scr[...] = jnp.zeros_like(xr_scr)
        xi_scr[...] = jnp.zeros_like(xi_scr)

    zeros = jnp.zeros((tt, hw), F32)
    for b in range(nb):
        ub = u_ref[b].astype(F32)
        for hh in range(hs):
            rows = pl.ds(hh * nb + b, tt, stride=SUBLANES)
            for c in range(hs):
                lhs_scr[c, rows, :] = ub[:, c * hw:(c + 1) * hw] if c == hh else zeros

    lhs = jnp.concatenate([lhs_scr[c] for c in range(hs)], axis=-1).astype(BF16)
    hm = lhs.shape[0] // 2
    for r0 in (0, hm):
        sr_scr[r0:r0 + hm, :] = jnp.dot(lhs[r0:r0 + hm], rb_ref[0, :, :sw],
                                        preferred_element_type=F32)
        si_scr[r0:r0 + hm, :] = jnp.dot(lhs[r0:r0 + hm], rb_ref[0, :, sw:],
                                        preferred_element_type=F32)

    ar, ai = ar_ref[0], ai_ref[0]

    def step(t, carry):
        xr, xi = carry
        rows = pl.ds(pl.multiple_of(t * SUBLANES, SUBLANES), SUBLANES)
        nxr = ar * xr - ai * xi + sr_scr[rows, :]
        nxi = ar * xi + ai * xr + si_scr[rows, :]
        sr_scr[rows, :] = nxr
        si_scr[rows, :] = nxi
        return nxr, nxi

    xr, xi = lax.fori_loop(0, tt, step, (xr_scr[...], xi_scr[...]), unroll=8)
    xr_scr[...] = xr
    xi_scr[...] = xi

    for r0 in (0, hm):
        o2 = (jnp.dot(sr_scr[r0:r0 + hm, :].astype(BF16), rc_ref[0, :sw, :],
                      preferred_element_type=F32)
              + jnp.dot(si_scr[r0:r0 + hm, :].astype(BF16), rc_ref[0, sw:, :],
                        preferred_element_type=F32))
        for c in range(hs):
            o2_scr[c, r0:r0 + hm, :] = o2[:, c * hw:(c + 1) * hw]

    for b in range(nb):
        parts = [o2_scr[hh, pl.ds(hh * nb + b, tt, stride=SUBLANES), :] for hh in range(hs)]
        yb = jnp.concatenate(parts, axis=-1) + d_ref[...] * u_ref[b].astype(F32)
        y_ref[b] = _gelu_tanh(yb).astype(y_ref.dtype)


def s5_core(z3, abar_r, abar_i, bbar_r, bbar_i, c_re, c_im, d_skip, *, tt=256, side_casts=None):
    nb, seq, _ = z3.shape
    g, n, p = bbar_r.shape
    assert SUBLANES % nb == 0
    hs = SUBLANES // nb
    gph = GROUPS_PER_HALF
    assert g % (hs * gph) == 0
    nblk = g // (hs * gph)
    hw, sw = gph * p, gph * n
    cw = hs * hw
    assert hw == LANES and sw % LANES == 0
    tt = min(tt, seq)
    assert seq % tt == 0

    eye = jnp.eye(gph, dtype=F32)

    def in_proj(bb):
        bb = bb.reshape(nblk, hs, gph, n, p)
        return jnp.einsum("jhqnp,qr->jhqprn", bb, eye).reshape(nblk, cw, sw)

    def out_proj(cc):
        cc = cc.reshape(nblk, hs, gph, p, n)
        return jnp.einsum("jhrpn,qr->jqnhrp", cc, eye).reshape(nblk, sw, cw)

    rb = jnp.concatenate([in_proj(bbar_r), in_proj(bbar_i)], axis=-1).astype(BF16)
    rc = jnp.concatenate([out_proj(c_re), out_proj(-c_im)], axis=1).astype(BF16)

    def rows(a):
        a = a.reshape(nblk, hs, 1, sw)
        return jnp.broadcast_to(a, (nblk, hs, nb, sw)).reshape(nblk, hs * nb, sw)

    (y,), side = _pallas_with_side_casts(
        functools.partial(_s5_kernel, nb=nb, hs=hs, tt=tt, sw=sw, hw=hw),
        grid=(nblk, seq // tt),
        in_specs=[pl.BlockSpec((nb, tt, cw), lambda j, t: (0, t, j)),
                  pl.BlockSpec((1, cw, 2 * sw), lambda j, t: (j, 0, 0)),
                  pl.BlockSpec((1, 2 * sw, cw), lambda j, t: (j, 0, 0)),
                  pl.BlockSpec((1, SUBLANES, sw), lambda j, t: (j, 0, 0)),
                  pl.BlockSpec((1, SUBLANES, sw), lambda j, t: (j, 0, 0)),
                  pl.BlockSpec((1, cw), lambda j, t: (0, j))],
        out_specs=[pl.BlockSpec((nb, tt, cw), lambda j, t: (0, t, j))],
        out_shape=[jax.ShapeDtypeStruct((nb, seq, g * p), BF16)],
        args=[z3, rb, rc, rows(abar_r), rows(abar_i), d_skip.reshape(1, g * p).astype(F32)],
        side_casts=side_casts or (),
        scratch_shapes=[pltpu.VMEM((hs, SUBLANES * tt, hw), F32),
                        pltpu.VMEM((SUBLANES * tt, sw), F32),
                        pltpu.VMEM((SUBLANES * tt, sw), F32),
                        pltpu.VMEM((hs, SUBLANES * tt, hw), F32),
                        pltpu.VMEM((SUBLANES, sw), F32),
                        pltpu.VMEM((SUBLANES, sw), F32)],
        compiler_params=_params("parallel", "arbitrary"),
        name="s5_core",
    )
    return y if side_casts is None else (y, *side)


def _glu_kernel(y_ref, w_ref, g_ref, o_ref):
    y = y_ref[...]
    v = jnp.dot(y, w_ref[...], preferred_element_type=F32)
    o = y.astype(F32) * jax.nn.sigmoid(v)
    o_ref[...] = _rms(o, g_ref[...]).astype(o_ref.dtype)


def glu_norm(y, w, g, *, out_cols, tm=512):
    m, k = y.shape
    tm = min(tm, m)
    return pl.pallas_call(
        _glu_kernel,
        grid=(m // tm,),
        in_specs=[pl.BlockSpec((tm, k), lambda i: (i, 0)),
                  pl.BlockSpec((k, k), lambda i: (0, 0)),
                  pl.BlockSpec((1, k), lambda i: (0, 0))],
        out_specs=pl.BlockSpec((tm, k), lambda i: (i, 0)),
        out_shape=jax.ShapeDtypeStruct((m, out_cols), BF16),
        compiler_params=_params("parallel"),
        name="glu_norm",
    )(y, w, g.reshape(1, k).astype(F32))


LOG2E = math.log2(math.e)
def _cumsum_kernel(f_ref, bf_ref, c_ref, *, chunk):
    seq = f_ref.shape[1]
    row = lax.broadcasted_iota(jnp.int32, (chunk, chunk), 0)
    col = lax.broadcasted_iota(jnp.int32, (chunk, chunk), 1)
    tri = (col <= row).astype(F32)
    carry = jnp.zeros((1, f_ref.shape[2]), F32)
    for ci in range(seq // chunk):
        sl = slice(ci * chunk, (ci + 1) * chunk)
        x = f_ref[0, sl, :] + bf_ref[...]
        log_f = (jnp.minimum(x, 0.0) - jnp.log1p(jnp.exp(-jnp.abs(x)))) * LOG2E
        cs = jnp.dot(tri, log_f, precision=lax.Precision.HIGHEST,
                     preferred_element_type=F32) + carry
        c_ref[0, sl, :] = cs
        carry = cs[chunk - 1:chunk, :]


def forget_cumsum(f3, b_f_row, *, chunk=256):
    nb, seq, w = f3.shape
    chunk = min(chunk, seq)
    return pl.pallas_call(
        functools.partial(_cumsum_kernel, chunk=chunk),
        grid=(nb,),
        in_specs=[pl.BlockSpec((1, seq, w), lambda b: (b, 0, 0)),
                  pl.BlockSpec((1, w), lambda b: (0, 0))],
        out_specs=pl.BlockSpec((1, seq, w), lambda b: (b, 0, 0)),
        out_shape=jax.ShapeDtypeStruct((nb, seq, w), F32),
        compiler_params=_params("parallel"),
        name="forget_cumsum",
    )(f3, b_f_row)


FOX_HEADS_PER_STEP = 2


def _fox_kernel(q_ref, k_ref, v_ref, cq_ref, ck_ref, o_ref, *, tq, big, nh, dh):
    i = pl.program_id(2)
    per_big = big // tq

    def scores(hd, start, width, diagonal):
        cols = slice(hd * dh, (hd + 1) * dh)
        rows = pl.ds(start, width)
        s = lax.dot_general(q_ref[0, :, cols], k_ref[0, rows, cols], (((1,), (1,)), ((), ())),
                            preferred_element_type=F32)
        s = s + (cq_ref[0, hd] - ck_ref[0, hd, :, rows])
        if diagonal:
            r = lax.broadcasted_iota(jnp.int32, s.shape, 0)
            c = lax.broadcasted_iota(jnp.int32, s.shape, 1)
            s = jnp.where(c <= r, s, NEG_BIG)
        return s

    def update(hd, s, start, width, carry):
        m, l, acc = carry
        v = v_ref[0, pl.ds(start, width), hd * dh:(hd + 1) * dh]
        m_new = jnp.maximum(m, jnp.max(s, axis=-1, keepdims=True))
        alpha = jnp.exp2(m - m_new)
        p = jnp.exp2(s - m_new)
        l = alpha * l + jnp.sum(p, axis=-1, keepdims=True)
        acc = alpha * acc + jnp.dot(p.astype(v.dtype), v, preferred_element_type=F32)
        return m_new, l, acc

    def block(start, width, carries, diagonal):
        ss = [scores(hd, start, width, diagonal) for hd in range(nh)]
        return tuple(update(hd, ss[hd], start, width, carries[hd]) for hd in range(nh))

    init = tuple((jnp.full((tq, 1), NEG_BIG, F32), jnp.zeros((tq, 1), F32),
                  jnp.zeros((tq, dh), F32)) for _ in range(nh))
    n_big = i // per_big
    carries = lax.fori_loop(
        0, n_big, lambda j, c: block(pl.multiple_of(j * big, big), big, c, False), init)
    carries = lax.fori_loop(
        n_big * per_big, i, lambda j, c: block(pl.multiple_of(j * tq, tq), tq, c, False), carries)
    carries = block(pl.multiple_of(i * tq, tq), tq, carries, True)
    for hd in range(nh):
        _, l, acc = carries[hd]
        o_ref[0, :, hd * dh:(hd + 1) * dh] = (acc / l).astype(o_ref.dtype)


def fox_attention(z3, c, *, q_off, k_off, v_off, heads, dh, tq=1024, big=1024, side_casts=None):
    nb, seq, _ = z3.shape
    tq = min(tq, seq)
    big = min(big, seq)
    nq = seq // tq
    nh = FOX_HEADS_PER_STEP if heads % FOX_HEADS_PER_STEP == 0 else 1
    w = nh * dh
    assert seq % tq == 0 and big % tq == 0
    assert q_off % w == 0 and k_off % w == 0 and v_off % w == 0
    ch = jnp.transpose(c[:, :, :heads], (0, 2, 1))
    cq = ch.reshape(nb, heads, seq, 1)
    ck = ch.reshape(nb, heads, 1, seq)
    qb, kb, vb = q_off // w, k_off // w, v_off // w
    (out,), side = _pallas_with_side_casts(
        functools.partial(_fox_kernel, tq=tq, big=big, nh=nh, dh=dh),
        grid=(nb, heads // nh, nq),
        in_specs=[pl.BlockSpec((1, tq, w), lambda b, h, i: (b, i, qb + h)),
                  pl.BlockSpec((1, seq, w), lambda b, h, i: (b, 0, kb + h)),
                  pl.BlockSpec((1, seq, w), lambda b, h, i: (b, 0, vb + h)),
                  pl.BlockSpec((1, nh, tq, 1), lambda b, h, i: (b, h, i, 0)),
                  pl.BlockSpec((1, nh, 1, seq), lambda b, h, i: (b, h, 0, 0))],
        out_specs=[pl.BlockSpec((1, tq, w), lambda b, h, i: (b, i, h))],
        out_shape=[jax.ShapeDtypeStruct((nb, seq, heads * dh), F32)],
        args=[z3, z3, z3, cq, ck],
        side_casts=side_casts or (),
        compiler_params=_params("parallel", "parallel", "parallel"),
        name="fox_attention",
    )
    return out if side_casts is None else (out, *side)


def _xattn_kernel(q_ref, k_ref, v_ref, o_ref, *, heads, dh):
    scores = [lax.dot_general(q_ref[:, hd * dh:(hd + 1) * dh], k_ref[:, hd * dh:(hd + 1) * dh],
                              (((1,), (1,)), ((), ())), preferred_element_type=F32)
              for hd in range(heads)]
    for hd in range(heads):
        sl = slice(hd * dh, (hd + 1) * dh)
        s = scores[hd]
        s = s - jnp.max(s, axis=-1, keepdims=True)
        p = jnp.exp(s)
        p = p / jnp.sum(p, axis=-1, keepdims=True)
        o_ref[:, sl] = jnp.dot(p.astype(BF16), v_ref[:, sl],
                               preferred_element_type=F32).astype(o_ref.dtype)


def cross_attention(q, k, v, *, nb, heads, tm=512):
    t, w = q.shape
    seq, mlen = t // nb, k.shape[0] // nb
    tm = min(tm, seq)
    nt = seq // tm
    return pl.pallas_call(
        functools.partial(_xattn_kernel, heads=heads, dh=w // heads),
        grid=(nb, nt),
        in_specs=[pl.BlockSpec((tm, w), lambda b, i: (b * nt + i, 0)),
                  pl.BlockSpec((mlen, w), lambda b, i: (b, 0)),
                  pl.BlockSpec((mlen, w), lambda b, i: (b, 0))],
        out_specs=pl.BlockSpec((tm, w), lambda b, i: (b * nt + i, 0)),
        out_shape=jax.ShapeDtypeStruct((t, w), BF16),
        compiler_params=_params("parallel", "parallel"),
        name="cross_attention",
    )(q, k, v)


def kernel(x, mem, ffn1_norm, ffn1_w_gate, ffn1_w_up, ffn1_w_down, mix_norm, w_in, ssm_A_re, ssm_A_im, ssm_log_step, ssm_B_re, ssm_B_im, ssm_C_re, ssm_C_im, ssm_D, ssm_w_glu, ssm_out_norm, fox_b_f, fox_q_norm, fox_k_norm, fox_out_norm, w_out, xattn_norm, mem_norm, xattn_wq, xattn_wk, xattn_wv, xattn_q_norm, xattn_k_norm, xattn_wo, ffn2_norm, ffn2_w_gate, ffn2_w_up, ffn2_w_down, final_norm):
    nb, seq, d = x.shape
    t = nb * seq
    depth = ffn1_norm.shape[0]
    g, n_state = ssm_A_re.shape[1:]
    p = ssm_B_re.shape[-1]
    w_ssm = g * p
    fox_h, fox_dh = fox_b_f.shape[-1], fox_q_norm.shape[-1]
    w_fox = fox_h * fox_dh
    o_q, o_k, o_v, o_f = w_ssm, w_ssm + w_fox, w_ssm + 2 * w_fox, w_ssm + 3 * w_fox
    x_dh = xattn_q_norm.shape[-1]
    x_w = xattn_wq.shape[-1]
    x_heads = x_w // x_dh
    mlen = mem.shape[1]
    tn_in = min(1024, w_ssm, w_fox)
    assert w_ssm % w_fox == 0

    h = x.reshape(t, d)
    mem2 = mem.reshape(nb * mlen, d)

    ffn1_w = (cast_bf16(ffn1_w_gate, 0), cast_bf16(ffn1_w_up, 0),
              cast_bf16(ffn1_w_down, 0, scale=0.5))

    for l in range(depth):
        more = l + 1 < depth
        h = ffn(h, ffn1_norm[l], *ffn1_w)

        hn = rmsnorm(h, mix_norm[l], BF16)
        w_in_b = cast_bf16(w_in, l)
        gain_in = jnp.concatenate([
            jnp.ones((w_ssm,), F32),
            jnp.tile(fox_q_norm[l].astype(F32) * (fox_dh ** -0.5 * LOG2E), fox_h),
            jnp.tile(fox_k_norm[l].astype(F32), fox_h),
            jnp.ones((w_fox,), F32)])
        z, *ffn2_w = matmul(hn, w_in_b, BF16, tm=1024, tn=tn_in, n_cols=o_f, norm_gain=gain_in,
                            norm_group=fox_dh, norm_cols=(o_q, o_v),
                            side_casts=[(ffn2_w_gate, l, 1.0), (ffn2_w_up, l, 1.0),
                                        (ffn2_w_down, l, 0.5)])
        w_f = jnp.pad(w_in_b[:, o_f:], ((0, 0), (0, LANES - fox_h)))
        f_logit = matmul(hn, w_f, F32, tm=1024, tn=LANES)
        z3 = z.reshape(nb, seq, o_f)

        abar_r, abar_i, bbar_r, bbar_i = s5_discretise(
            ssm_A_re[l], ssm_A_im[l], ssm_log_step[l], ssm_B_re[l], ssm_B_im[l])
        y_ssm, *next_gate_up = s5_core(
            z3, abar_r, abar_i, bbar_r, bbar_i,
            ssm_C_re[l].astype(F32), ssm_C_im[l].astype(F32), ssm_D[l],
            side_casts=[(ffn1_w_gate, l + 1, 1.0), (ffn1_w_up, l + 1, 1.0)] if more else [])
        y = glu_norm(y_ssm.reshape(t, w_ssm), cast_bf16(ssm_w_glu, l), ssm_out_norm[l],
                     out_cols=w_ssm + w_fox)

        b_f_row = jnp.pad(fox_b_f[l].astype(F32), (0, LANES - fox_h)).reshape(1, LANES)
        c = forget_cumsum(f_logit.reshape(nb, seq, LANES), b_f_row)
        y_fox, *next_down = fox_attention(
            z3, c, q_off=o_q, k_off=o_k, v_off=o_v, heads=fox_h, dh=fox_dh,
            side_casts=[(ffn1_w_down, l + 1, 0.5)] if more else [])
        ffn1_w = (*next_gate_up, *next_down)
        y = rmsnorm(y_fox.reshape(t, w_fox), fox_out_norm[l], BF16, into=y,
                    col_block=w_ssm // w_fox)

        h = matmul(y, cast_bf16(w_out, l), F32, tm=1024, tn=1024, res=h)

        hn = rmsnorm(h, xattn_norm[l], BF16)
        mn = rmsnorm(mem2, mem_norm[l], BF16)
        q = matmul(hn, cast_bf16(xattn_wq, l), BF16, tm=1024, tn=x_dh,
                   norm_gain=jnp.tile(xattn_q_norm[l].astype(F32) * (x_dh ** -0.5), x_heads),
                   norm_group=x_dh)
        k = matmul(mn, cast_bf16(xattn_wk, l), BF16, tm=1024, tn=x_dh,
                   norm_gain=jnp.tile(xattn_k_norm[l].astype(F32), x_heads), norm_group=x_dh)
        v = matmul(mn, cast_bf16(xattn_wv, l), BF16, tm=1024, tn=x_dh)
        o = cross_attention(q, k, v, nb=nb, heads=x_heads)
        h = matmul(o, cast_bf16(xattn_wo, l), F32, tm=1024, tn=1024, res=h)

        h = ffn(h, ffn2_norm[l], *ffn2_w, final_gain=final_norm[l])

    return h.reshape(nb, seq, d)
```

```python
import functools
import math

import jax
import jax.numpy as jnp
from jax import lax
from jax.experimental import pallas as pl
from jax.experimental.pallas import tpu as pltpu

F32 = jnp.float32
BF16 = jnp.bfloat16
EPS = 1e-6
NEG_BIG = -1e30

V7X_VMEM_LIMIT_BYTES = 58 * 1024 * 1024
LANES = 128
SUBLANES = 8


def _params(*sem):
    return pltpu.CompilerParams(dimension_semantics=sem,
                                vmem_limit_bytes=V7X_VMEM_LIMIT_BYTES)


def _rms(x, g):
    ms = jnp.mean(x * x, axis=-1, keepdims=True)
    return x * lax.rsqrt(ms + EPS) * g


def _cast_kernel(w_ref, o_ref, *, scale):
    w = w_ref[...]
    if scale != 1.0:
        w = w * scale
    o_ref[...] = w.astype(o_ref.dtype)


def cast_bf16(w_stack, layer, *, scale=1.0, tr=256):
    _, k, n = w_stack.shape
    tr = min(tr, k)
    assert k % tr == 0
    return pl.pallas_call(
        functools.partial(_cast_kernel, scale=scale),
        grid=(k // tr,),
        in_specs=[pl.BlockSpec((None, tr, n), lambda i: (layer, i, 0))],
        out_specs=pl.BlockSpec((tr, n), lambda i: (i, 0)),
        out_shape=jax.ShapeDtypeStruct((k, n), BF16),
        compiler_params=_params("parallel"),
        name="cast_bf16",
    )(w_stack)

BF16_SUBLANE_TILE = 16


def _side_block(k, n, steps):
    if k % steps == 0 and (k // steps) % BF16_SUBLANE_TILE == 0:
        return (k // steps, n), lambda s: (s, 0)
    for s1 in (16, 8, 4, 2, 1):
        s2 = steps // s1
        if (steps % s1 == 0 and k % s1 == 0 and (k // s1) % BF16_SUBLANE_TILE == 0
                and n % s2 == 0 and (n // s2) % LANES == 0):
            return (k // s1, n // s2), lambda s: (s // s2, s % s2)
    raise ValueError(f"cannot tile a ({k}, {n}) weight in {steps} blocks")


def _pallas_with_side_casts(body, *, grid, in_specs, out_specs, out_shape, args, side_casts=(),
                            **kwargs):
    out_specs, out_shape = list(out_specs), list(out_shape)
    n_in, n_out, n_side = len(in_specs), len(out_specs), len(side_casts)
    steps = math.prod(grid)
    strides = [math.prod(grid[a + 1:]) for a in range(len(grid))]
    scales = []
    for w_stack, layer, scale in side_casts:
        _, k, n = w_stack.shape
        blk, to_index = _side_block(k, n, steps)

        def index(*g, _to_index=to_index):
            return _to_index(sum(gi * st for gi, st in zip(g, strides)))

        in_specs = in_specs + [pl.BlockSpec((None,) + blk,
                                            lambda *g, _l=layer, _ix=index: (_l,) + _ix(*g))]
        out_specs.append(pl.BlockSpec(blk, index))
        out_shape.append(jax.ShapeDtypeStruct((k, n), BF16))
        args = list(args) + [w_stack]
        scales.append(scale)

    def kern(*refs):
        side_in = refs[n_in:n_in + n_side]
        outs = refs[n_in + n_side:n_in + n_side + n_out]
        side_out = refs[n_in + n_side + n_out:n_in + 2 * n_side + n_out]
        scratch = refs[n_in + 2 * n_side + n_out:]
        for si, so, scale in zip(side_in, side_out, scales):
            w = si[...]
            so[...] = (w if scale == 1.0 else w * scale).astype(so.dtype)
        body(*refs[:n_in], *outs, *scratch)

    res = pl.pallas_call(kern, grid=grid, in_specs=in_specs, out_specs=out_specs,
                         out_shape=out_shape, **kwargs)(*args)
    return tuple(res[:n_out]), tuple(res[n_out:])


def _rmsnorm_kernel(x_ref, g_ref, *rest):
    o_ref = rest[-1]
    o_ref[...] = _rms(x_ref[...].astype(F32), g_ref[...]).astype(o_ref.dtype)


def rmsnorm(x, g, out_dtype, tm=256, into=None, col_block=0):
    m, k = x.shape
    tm = min(tm, m)
    in_specs = [pl.BlockSpec((tm, k), lambda i: (i, 0)),
                pl.BlockSpec((1, k), lambda i: (0, 0))]
    args = [x, g.reshape(1, k).astype(F32)]
    aliases = {}
    out_shape = jax.ShapeDtypeStruct((m, k), out_dtype)
    if into is not None:
        assert into.shape[0] == m and into.shape[1] % k == 0 and into.dtype == out_dtype
        in_specs.append(pl.BlockSpec(memory_space=pl.ANY))
        args.append(into)
        aliases = {2: 0}
        out_shape = jax.ShapeDtypeStruct(into.shape, into.dtype)
    return pl.pallas_call(
        _rmsnorm_kernel,
        grid=(m // tm,),
        in_specs=in_specs,
        out_specs=pl.BlockSpec((tm, k), lambda i: (i, col_block)),
        out_shape=out_shape,
        input_output_aliases=aliases,
        compiler_params=_params("parallel"),
        name="rmsnorm",
    )(*args)


def _mm_kernel(*refs, has_res, norm_group, norm_lo, norm_hi):
    x_ref, w_ref = refs[0], refs[1]
    pos = 2
    res_ref = g_ref = None
    if has_res:
        res_ref = refs[pos]
        pos += 1
    if norm_group:
        g_ref = refs[pos]
        pos += 1
    o_ref = refs[pos]

    acc = jnp.dot(x_ref[...], w_ref[...], preferred_element_type=F32)
    if has_res:
        acc = acc + res_ref[...]
    if not norm_group:
        o_ref[...] = acc.astype(o_ref.dtype)
        return

    j = pl.program_id(1)
    in_range = jnp.logical_and(j >= norm_lo, j < norm_hi)

    @pl.when(in_range)
    def _():
        for c in range(acc.shape[1] // norm_group):
            sl = slice(c * norm_group, (c + 1) * norm_group)
            o_ref[:, sl] = _rms(acc[:, sl], g_ref[:, sl]).astype(o_ref.dtype)

    @pl.when(jnp.logical_not(in_range))
    def _():
        o_ref[...] = acc.astype(o_ref.dtype)


def matmul(x, w, out_dtype, *, tm=512, tn=512, n_cols=None, res=None,
           norm_gain=None, norm_group=0, norm_cols=None, side_casts=None):
    m, k = x.shape
    n = w.shape[1] if n_cols is None else n_cols
    tm, tn = min(tm, m), min(tn, n)
    assert m % tm == 0 and n % tn == 0
    in_specs = [pl.BlockSpec((tm, k), lambda i, j: (i, 0)),
                pl.BlockSpec((k, tn), lambda i, j: (0, j))]
    args = [x, w]
    if res is not None:
        in_specs.append(pl.BlockSpec((tm, tn), lambda i, j: (i, j)))
        args.append(res)
    norm_lo = norm_hi = 0
    if norm_group:
        lo, hi = (0, n) if norm_cols is None else norm_cols
        assert tn % norm_group == 0 and lo % tn == 0 and hi % tn == 0
        norm_lo, norm_hi = lo // tn, hi // tn
        in_specs.append(pl.BlockSpec((1, tn), lambda i, j: (0, j)))
        args.append(norm_gain.reshape(1, n).astype(F32))
    (out,), side = _pallas_with_side_casts(
        functools.partial(_mm_kernel, has_res=res is not None, norm_group=norm_group,
                          norm_lo=norm_lo, norm_hi=norm_hi),
        grid=(m // tm, n // tn),
        in_specs=in_specs,
        out_specs=[pl.BlockSpec((tm, tn), lambda i, j: (i, j))],
        out_shape=[jax.ShapeDtypeStruct((m, n), out_dtype)],
        args=args,
        side_casts=side_casts or (),
        compiler_params=_params("parallel", "parallel"),
        name="matmul",
    )
    return out if side_casts is None else (out, *side)


def _ffn_kernel(*refs, final_norm):
    if final_norm:
        x_ref, g_ref, wg_ref, wu_ref, wd_ref, fg_ref, o_ref, n_scr = refs
    else:
        x_ref, g_ref, wg_ref, wu_ref, wd_ref, o_ref, n_scr = refs
        fg_ref = None
    j = pl.program_id(1)

    @pl.when(j == 0)
    def _():
        x = x_ref[...]
        n_scr[...] = _rms(x, g_ref[...]).astype(BF16)
        o_ref[...] = x

    n = n_scr[...]
    gate = jnp.dot(n, wg_ref[...], preferred_element_type=F32)
    up = jnp.dot(n, wu_ref[...], preferred_element_type=F32)
    hid = (gate * jax.nn.sigmoid(gate) * up).astype(BF16)
    o_ref[...] += jnp.dot(hid, wd_ref[...], preferred_element_type=F32)

    if final_norm:
        @pl.when(j == pl.num_programs(1) - 1)
        def _():
            o_ref[...] = _rms(o_ref[...], fg_ref[...])


def ffn(x, g, wg, wu, wd_half, final_gain=None, *, tm=512, tf=256):
    m, d = x.shape
    dff = wg.shape[1]
    tm, tf = min(tm, m), min(tf, dff)
    assert m % tm == 0 and dff % tf == 0
    in_specs = [pl.BlockSpec((tm, d), lambda i, j: (i, 0)),
                pl.BlockSpec((1, d), lambda i, j: (0, 0)),
                pl.BlockSpec((d, tf), lambda i, j: (0, j)),
                pl.BlockSpec((d, tf), lambda i, j: (0, j)),
                pl.BlockSpec((tf, d), lambda i, j: (j, 0))]
    args = [x, g.reshape(1, d).astype(F32), wg, wu, wd_half]
    if final_gain is not None:
        in_specs.append(pl.BlockSpec((1, d), lambda i, j: (0, 0)))
        args.append(final_gain.reshape(1, d).astype(F32))
    return pl.pallas_call(
        functools.partial(_ffn_kernel, final_norm=final_gain is not None),
        grid=(m // tm, dff // tf),
        in_specs=in_specs,
        out_specs=pl.BlockSpec((tm, d), lambda i, j: (i, 0)),
        out_shape=jax.ShapeDtypeStruct((m, d), F32),
        scratch_shapes=[pltpu.VMEM((tm, d), BF16)],
        compiler_params=_params("parallel", "arbitrary"),
        name="ffn",
    )(*args)


def _s5_disc_kernel(are_ref, aim_ref, ls_ref, bre_ref, bim_ref,
                    abr_ref, abi_ref, bbr_ref, bbi_ref):
    ar, ai = are_ref[...], aim_ref[...]
    dt = jnp.exp(ls_ref[...])
    mag = jnp.exp(ar * dt)
    abar_r = mag * jnp.cos(ai * dt)
    abar_i = mag * jnp.sin(ai * dt)
    den = ar * ar + ai * ai
    pr, pi_ = abar_r - 1.0, abar_i
    coef_r = (pr * ar + pi_ * ai) / den
    coef_i = (pi_ * ar - pr * ai) / den
    br, bi = bre_ref[...], bim_ref[...]
    abr_ref[...] = abar_r
    abi_ref[...] = abar_i
    bbr_ref[...] = coef_r * br - coef_i * bi
    bbi_ref[...] = coef_r * bi + coef_i * br


def s5_discretise(a_re, a_im, log_step, b_re, b_im):
    g, n = a_re.shape
    p = b_re.shape[-1]
    col = jax.ShapeDtypeStruct((g, n, 1), F32)
    mat = jax.ShapeDtypeStruct((g, n, p), F32)
    ls = jnp.broadcast_to(log_step.reshape(g, 1, 1), (g, n, 1))
    abr, abi, bbr, bbi = pl.pallas_call(
        _s5_disc_kernel,
        out_shape=(col, col, mat, mat),
        compiler_params=pltpu.CompilerParams(vmem_limit_bytes=V7X_VMEM_LIMIT_BYTES),
        name="s5_discretise",
    )(a_re.reshape(g, n, 1), a_im.reshape(g, n, 1), ls, b_re, b_im)
    return abr.reshape(g, n), abi.reshape(g, n), bbr, bbi


GROUPS_PER_HALF = 8


def _gelu_tanh(x):
    c = math.sqrt(2.0 / math.pi)
    return 0.5 * x * (1.0 + jnp.tanh(c * (x + 0.044715 * (x * x * x))))


def _s5_kernel(u_ref, rb_ref, rc_ref, ar_ref, ai_ref, d_ref, y_ref,
               lhs_scr, sr_scr, si_scr, o2_scr, xr_scr, xi_scr, *, nb, hs, tt, sw, hw):
    @pl.when(pl.program_id(1) == 0)
    def _():
        xr_scr[...] = jnp.zeros_like(xr_scr)
        xi_scr[...] = jnp.zeros_like(xi_scr)

    zeros = jnp.zeros((tt, hw), F32)
    for b in range(nb):
        ub = u_ref[b].astype(F32)
        for hh in range(hs):
            rows = pl.ds(hh * nb + b, tt, stride=SUBLANES)
            for c in range(hs):
                lhs_scr[c, rows, :] = ub[:, c * hw:(c + 1) * hw] if c == hh else zeros

    lhs = jnp.concatenate([lhs_scr[c] for c in range(hs)], axis=-1).astype(BF16)
    hm = lhs.shape[0] // 2
    for r0 in (0, hm):
        sr_scr[r0:r0 + hm, :] = jnp.dot(lhs[r0:r0 + hm], rb_ref[0, :, :sw],
                                        preferred_element_type=F32)
        si_scr[r0:r0 + hm, :] = jnp.dot(lhs[r0:r0 + hm], rb_ref[0, :, sw:],
                                        preferred_element_type=F32)

    ar, ai = ar_ref[0], ai_ref[0]

    def step(t, carry):
        xr, xi = carry
        rows = pl.ds(pl.multiple_of(t * SUBLANES, SUBLANES), SUBLANES)
        nxr = ar * xr - ai * xi + sr_scr[rows, :]
        nxi = ar * xi + ai * xr + si_scr[rows, :]
        sr_scr[rows, :] = nxr
        si_scr[rows, :] = nxi
        return nxr, nxi

    xr, xi = lax.fori_loop(0, tt, step, (xr_scr[...], xi_scr[...]), unroll=8)
    xr_scr[...] = xr
    xi_scr[...] = xi

    for r0 in (0, hm):
        o2 = (jnp.dot(sr_scr[r0:r0 + hm, :].astype(BF16), rc_ref[0, :sw, :],
                      preferred_element_type=F32)
              + jnp.dot(si_scr[r0:r0 + hm, :].astype(BF16), rc_ref[0, sw:, :],
                        preferred_element_type=F32))
        for c in range(hs):
            o2_scr[c, r0:r0 + hm, :] = o2[:, c * hw:(c + 1) * hw]

    for b in range(nb):
        parts = [o2_scr[hh, pl.ds(hh * nb + b, tt, stride=SUBLANES), :] for hh in range(hs)]
        yb = jnp.concatenate(parts, axis=-1) + d_ref[...] * u_ref[b].astype(F32)
        y_ref[b] = _gelu_tanh(yb).astype(y_ref.dtype)


def s5_core(z3, abar_r, abar_i, bbar_r, bbar_i, c_re, c_im, d_skip, *, tt=256, side_casts=None):
    nb, seq, _ = z3.shape
    g, n, p = bbar_r.shape
    assert SUBLANES % nb == 0
    hs = SUBLANES // nb
    gph = GROUPS_PER_HALF
    assert g % (hs * gph) == 0
    nblk = g // (hs * gph)
    hw, sw = gph * p, gph * n
    cw = hs * hw
    assert hw == LANES and sw % LANES == 0
    tt = min(tt, seq)
    assert seq % tt == 0

    eye = jnp.eye(gph, dtype=F32)

    def in_proj(bb):
        bb = bb.reshape(nblk, hs, gph, n, p)
        return jnp.einsum("jhqnp,qr->jhqprn", bb, eye).reshape(nblk, cw, sw)

    def out_proj(cc):
        cc = cc.reshape(nblk, hs, gph, p, n)
        return jnp.einsum("jhrpn,qr->jqnhrp", cc, eye).reshape(nblk, sw, cw)

    rb = jnp.concatenate([in_proj(bbar_r), in_proj(bbar_i)], axis=-1).astype(BF16)
    rc = jnp.concatenate([out_proj(c_re), out_proj(-c_im)], axis=1).astype(BF16)

    def rows(a):
        a = a.reshape(nblk, hs, 1, sw)
        return jnp.broadcast_to(a, (nblk, hs, nb, sw)).reshape(nblk, hs * nb, sw)

    (y,), side = _pallas_with_side_casts(
        functools.partial(_s5_kernel, nb=nb, hs=hs, tt=tt, sw=sw, hw=hw),
        grid=(nblk, seq // tt),
        in_specs=[pl.BlockSpec((nb, tt, cw), lambda j, t: (0, t, j)),
                  pl.BlockSpec((1, cw, 2 * sw), lambda j, t: (j, 0, 0)),
                  pl.BlockSpec((1, 2 * sw, cw), lambda j, t: (j, 0, 0)),
                  pl.BlockSpec((1, SUBLANES, sw), lambda j, t: (j, 0, 0)),
                  pl.BlockSpec((1, SUBLANES, sw), lambda j, t: (j, 0, 0)),
                  pl.BlockSpec((1, cw), lambda j, t: (0, j))],
        out_specs=[pl.BlockSpec((nb, tt, cw), lambda j, t: (0, t, j))],
        out_shape=[jax.ShapeDtypeStruct((nb, seq, g * p), BF16)],
        args=[z3, rb, rc, rows(abar_r), rows(abar_i), d_skip.reshape(1, g * p).astype(F32)],
        side_casts=side_casts or (),
        scratch_shapes=[pltpu.VMEM((hs, SUBLANES * tt, hw), F32),
                        pltpu.VMEM((SUBLANES * tt, sw), F32),
                        pltpu.VMEM((SUBLANES * tt, sw), F32),
                        pltpu.VMEM((hs, SUBLANES * tt, hw), F32),
                        pltpu.VMEM((SUBLANES, sw), F32),
                        pltpu.VMEM((SUBLANES, sw), F32)],
        compiler_params=_params("parallel", "arbitrary"),
        name="s5_core",
    )
    return y if side_casts is None else (y, *side)


def _glu_kernel(y_ref, w_ref, g_ref, o_ref):
    y = y_ref[...]
    v = jnp.dot(y, w_ref[...], preferred_element_type=F32)
    o = y.astype(F32) * jax.nn.sigmoid(v)
    o_ref[...] = _rms(o, g_ref[...]).astype(o_ref.dtype)


def glu_norm(y, w, g, *, out_cols, tm=512):
    m, k = y.shape
    tm = min(tm, m)
    return pl.pallas_call(
        _glu_kernel,
        grid=(m // tm,),
        in_specs=[pl.BlockSpec((tm, k), lambda i: (i, 0)),
                  pl.BlockSpec((k, k), lambda i: (0, 0)),
                  pl.BlockSpec((1, k), lambda i: (0, 0))],
        out_specs=pl.BlockSpec((tm, k), lambda i: (i, 0)),
        out_shape=jax.ShapeDtypeStruct((m, out_cols), BF16),
        compiler_params=_params("parallel"),
        name="glu_norm",
    )(y, w, g.reshape(1, k).astype(F32))


LOG2E = math.log2(math.e)
def _cumsum_kernel(f_ref, bf_ref, c_ref, *, chunk):
    seq = f_ref.shape[1]
    row = lax.broadcasted_iota(jnp.int32, (chunk, chunk), 0)
    col = lax.broadcasted_iota(jnp.int32, (chunk, chunk), 1)
    tri = (col <= row).astype(F32)
    carry = jnp.zeros((1, f_ref.shape[2]), F32)
    for ci in range(seq // chunk):
        sl = slice(ci * chunk, (ci + 1) * chunk)
        x = f_ref[0, sl, :] + bf_ref[...]
        log_f = (jnp.minimum(x, 0.0) - jnp.log1p(jnp.exp(-jnp.abs(x)))) * LOG2E
        cs = jnp.dot(tri, log_f, precision=lax.Precision.HIGHEST,
                     preferred_element_type=F32) + carry
        c_ref[0, sl, :] = cs
        carry = cs[chunk - 1:chunk, :]


def forget_cumsum(f3, b_f_row, *, chunk=256):
    nb, seq, w = f3.shape
    chunk = min(chunk, seq)
    return pl.pallas_call(
        functools.partial(_cumsum_kernel, chunk=chunk),
        grid=(nb,),
        in_specs=[pl.BlockSpec((1, seq, w), lambda b: (b, 0, 0)),
                  pl.BlockSpec((1, w), lambda b: (0, 0))],
        out_specs=pl.BlockSpec((1, seq, w), lambda b: (b, 0, 0)),
        out_shape=jax.ShapeDtypeStruct((nb, seq, w), F32),
        compiler_params=_params("parallel"),
        name="forget_cumsum",
    )(f3, b_f_row)


FOX_HEADS_PER_STEP = 2


def _fox_kernel(q_ref, k_ref, v_ref, cq_ref, ck_ref, o_ref, *, tq, big, nh, dh):
    i = pl.program_id(2)
    per_big = big // tq

    def scores(hd, start, width, diagonal):
        cols = slice(hd * dh, (hd + 1) * dh)
        rows = pl.ds(start, width)
        s = lax.dot_general(q_ref[0, :, cols], k_ref[0, rows, cols], (((1,), (1,)), ((), ())),
                            preferred_element_type=F32)
        s = s + (cq_ref[0, hd] - ck_ref[0, hd, :, rows])
        if diagonal:
            r = lax.broadcasted_iota(jnp.int32, s.shape, 0)
            c = lax.broadcasted_iota(jnp.int32, s.shape, 1)
            s = jnp.where(c <= r, s, NEG_BIG)
        return s

    def update(hd, s, start, width, carry):
        m, l, acc = carry
        v = v_ref[0, pl.ds(start, width), hd * dh:(hd + 1) * dh]
        m_new = jnp.maximum(m, jnp.max(s, axis=-1, keepdims=True))
        alpha = jnp.exp2(m - m_new)
        p = jnp.exp2(s - m_new)
        l = alpha * l + jnp.sum(p, axis=-1, keepdims=True)
        acc = alpha * acc + jnp.dot(p.astype(v.dtype), v, preferred_element_type=F32)
        return m_new, l, acc

    def block(start, width, carries, diagonal):
        ss = [scores(hd, start, width, diagonal) for hd in range(nh)]
        return tuple(update(hd, ss[hd], start, width, carries[hd]) for hd in range(nh))

    init = tuple((jnp.full((tq, 1), NEG_BIG, F32), jnp.zeros((tq, 1), F32),
                  jnp.zeros((tq, dh), F32)) for _ in range(nh))
    n_big = i // per_big
    carries = lax.fori_loop(
        0, n_big, lambda j, c: block(pl.multiple_of(j * big, big), big, c, False), init)
    carries = lax.fori_loop(
        n_big * per_big, i, lambda j, c: block(pl.multiple_of(j * tq, tq), tq, c, False), carries)
    carries = block(pl.multiple_of(i * tq, tq), tq, carries, True)
    for hd in range(nh):
        _, l, acc = carries[hd]
        o_ref[0, :, hd * dh:(hd + 1) * dh] = (acc / l).astype(o_ref.dtype)


def fox_attention(z3, c, *, q_off, k_off, v_off, heads, dh, tq=1024, big=1024, side_casts=None):
    nb, seq, _ = z3.shape
    tq = min(tq, seq)
    big = min(big, seq)
    nq = seq // tq
    nh = FOX_HEADS_PER_STEP if heads % FOX_HEADS_PER_STEP == 0 else 1
    w = nh * dh
    assert seq % tq == 0 and big % tq == 0
    assert q_off % w == 0 and k_off % w == 0 and v_off % w == 0
    ch = jnp.transpose(c[:, :, :heads], (0, 2, 1))
    cq = ch.reshape(nb, heads, seq, 1)
    ck = ch.reshape(nb, heads, 1, seq)
    qb, kb, vb = q_off // w, k_off // w, v_off // w
    (out,), side = _pallas_with_side_casts(
        functools.partial(_fox_kernel, tq=tq, big=big, nh=nh, dh=dh),
        grid=(nb, heads // nh, nq),
        in_specs=[pl.BlockSpec((1, tq, w), lambda b, h, i: (b, i, qb + h)),
                  pl.BlockSpec((1, seq, w), lambda b, h, i: (b, 0, kb + h)),
                  pl.BlockSpec((1, seq, w), lambda b, h, i: (b, 0, vb + h)),
                  pl.BlockSpec((1, nh, tq, 1), lambda b, h, i: (b, h, i, 0)),
                  pl.BlockSpec((1, nh, 1, seq), lambda b, h, i: (b, h, 0, 0))],
        out_specs=[pl.BlockSpec((1, tq, w), lambda b, h, i: (b, i, h))],
        out_shape=[jax.ShapeDtypeStruct((nb, seq, heads * dh), F32)],
        args=[z3, z3, z3, cq, ck],
        side_casts=side_casts or (),
        compiler_params=_params("parallel", "parallel", "parallel"),
        name="fox_attention",
    )
    return out if side_casts is None else (out, *side)


def _xattn_kernel(q_ref, k_ref, v_ref, o_ref, *, heads, dh):
    scores = [lax.dot_general(q_ref[:, hd * dh:(hd + 1) * dh], k_ref[:, hd * dh:(hd + 1) * dh],
                              (((1,), (1,)), ((), ())), preferred_element_type=F32)
              for hd in range(heads)]
    for hd in range(heads):
        sl = slice(hd * dh, (hd + 1) * dh)
        s = scores[hd]
        s = s - jnp.max(s, axis=-1, keepdims=True)
        p = jnp.exp(s)
        p = p / jnp.sum(p, axis=-1, keepdims=True)
        o_ref[:, sl] = jnp.dot(p.astype(BF16), v_ref[:, sl],
                               preferred_element_type=F32).astype(o_ref.dtype)


def cross_attention(q, k, v, *, nb, heads, tm=512):
    t, w = q.shape
    seq, mlen = t // nb, k.shape[0] // nb
    tm = min(tm, seq)
    nt = seq // tm
    return pl.pallas_call(
        functools.partial(_xattn_kernel, heads=heads, dh=w // heads),
        grid=(nb, nt),
        in_specs=[pl.BlockSpec((tm, w), lambda b, i: (b * nt + i, 0)),
                  pl.BlockSpec((mlen, w), lambda b, i: (b, 0)),
                  pl.BlockSpec((mlen, w), lambda b, i: (b, 0))],
        out_specs=pl.BlockSpec((tm, w), lambda b, i: (b * nt + i, 0)),
        out_shape=jax.ShapeDtypeStruct((t, w), BF16),
        compiler_params=_params("parallel", "parallel"),
        name="cross_attention",
    )(q, k, v)


def kernel(x, mem, ffn1_norm, ffn1_w_gate, ffn1_w_up, ffn1_w_down, mix_norm, w_in, ssm_A_re, ssm_A_im, ssm_log_step, ssm_B_re, ssm_B_im, ssm_C_re, ssm_C_im, ssm_D, ssm_w_glu, ssm_out_norm, fox_b_f, fox_q_norm, fox_k_norm, fox_out_norm, w_out, xattn_norm, mem_norm, xattn_wq, xattn_wk, xattn_wv, xattn_q_norm, xattn_k_norm, xattn_wo, ffn2_norm, ffn2_w_gate, ffn2_w_up, ffn2_w_down, final_norm):
    nb, seq, d = x.shape
    t = nb * seq
    depth = ffn1_norm.shape[0]
    g, n_state = ssm_A_re.shape[1:]
    p = ssm_B_re.shape[-1]
    w_ssm = g * p
    fox_h, fox_dh = fox_b_f.shape[-1], fox_q_norm.shape[-1]
    w_fox = fox_h * fox_dh
    o_q, o_k, o_v, o_f = w_ssm, w_ssm + w_fox, w_ssm + 2 * w_fox, w_ssm + 3 * w_fox
    x_dh = xattn_q_norm.shape[-1]
    x_w = xattn_wq.shape[-1]
    x_heads = x_w // x_dh
    mlen = mem.shape[1]
    tn_in = min(1024, w_ssm, w_fox)
    assert w_ssm % w_fox == 0

    h = x.reshape(t, d)
    mem2 = mem.reshape(nb * mlen, d)

    ffn1_w = (cast_bf16(ffn1_w_gate, 0), cast_bf16(ffn1_w_up, 0),
              cast_bf16(ffn1_w_down, 0, scale=0.5))
    w_in_b = cast_bf16(w_in, 0)

    for l in range(depth):
        more = l + 1 < depth
        h = ffn(h, ffn1_norm[l], *ffn1_w)

        hn = rmsnorm(h, mix_norm[l], BF16)
        gain_in = jnp.concatenate([
            jnp.ones((w_ssm,), F32),
            jnp.tile(fox_q_norm[l].astype(F32) * (fox_dh ** -0.5 * LOG2E), fox_h),
            jnp.tile(fox_k_norm[l].astype(F32), fox_h),
            jnp.ones((w_fox,), F32)])
        z, *ffn2_w = matmul(hn, w_in_b, BF16, tm=1024, tn=tn_in, n_cols=o_f, norm_gain=gain_in,
                            norm_group=fox_dh, norm_cols=(o_q, o_v),
                            side_casts=[(ffn2_w_gate, l, 1.0), (ffn2_w_up, l, 1.0),
                                        (ffn2_w_down, l, 0.5)])
        w_f = jnp.pad(w_in_b[:, o_f:], ((0, 0), (0, LANES - fox_h)))
        f_logit = matmul(hn, w_f, F32, tm=1024, tn=LANES)
        z3 = z.reshape(nb, seq, o_f)

        abar_r, abar_i, bbar_r, bbar_i = s5_discretise(
            ssm_A_re[l], ssm_A_im[l], ssm_log_step[l], ssm_B_re[l], ssm_B_im[l])
        y_ssm, w_glu_b, w_out_b, wq_b, wk_b, *next_gate_up = s5_core(
            z3, abar_r, abar_i, bbar_r, bbar_i,
            ssm_C_re[l].astype(F32), ssm_C_im[l].astype(F32), ssm_D[l],
            side_casts=[(ssm_w_glu, l, 1.0), (w_out, l, 1.0), (xattn_wq, l, 1.0),
                        (xattn_wk, l, 1.0)]
            + ([(ffn1_w_gate, l + 1, 1.0), (ffn1_w_up, l + 1, 1.0)] if more else []))
        y = glu_norm(y_ssm.reshape(t, w_ssm), w_glu_b, ssm_out_norm[l], out_cols=w_ssm + w_fox)

        b_f_row = jnp.pad(fox_b_f[l].astype(F32), (0, LANES - fox_h)).reshape(1, LANES)
        c = forget_cumsum(f_logit.reshape(nb, seq, LANES), b_f_row)
        y_fox, wv_b, wo_b, *next_rest = fox_attention(
            z3, c, q_off=o_q, k_off=o_k, v_off=o_v, heads=fox_h, dh=fox_dh,
            side_casts=[(xattn_wv, l, 1.0), (xattn_wo, l, 1.0)]
            + ([(ffn1_w_down, l + 1, 0.5), (w_in, l + 1, 1.0)] if more else []))
        if more:
            ffn1_w = (*next_gate_up, next_rest[0])
            w_in_b = next_rest[1]
        y = rmsnorm(y_fox.reshape(t, w_fox), fox_out_norm[l], BF16, into=y,
                    col_block=w_ssm // w_fox)

        h = matmul(y, w_out_b, F32, tm=1024, tn=1024, res=h)

        hn = rmsnorm(h, xattn_norm[l], BF16)
        mn = rmsnorm(mem2, mem_norm[l], BF16)
        q = matmul(hn, wq_b, BF16, tm=1024, tn=x_dh,
                   norm_gain=jnp.tile(xattn_q_norm[l].astype(F32) * (x_dh ** -0.5), x_heads),
                   norm_group=x_dh)
        k = matmul(mn, wk_b, BF16, tm=1024, tn=x_dh,
                   norm_gain=jnp.tile(xattn_k_norm[l].astype(F32), x_heads), norm_group=x_dh)
        v = matmul(mn, wv_b, BF16, tm=1024, tn=x_dh)
        o = cross_attention(q, k, v, nb=nb, heads=x_heads)
        h = matmul(o, wo_b, F32, tm=1024, tn=1024, res=h)

        h = ffn(h, ffn2_norm[l], *ffn2_w, final_gain=final_norm[l])

    return h.reshape(nb, seq, d)
```

```python
import functools
import math

import jax
import jax.numpy as jnp
from jax import lax
from jax.experimental import pallas as pl
from jax.experimental.pallas import tpu as pltpu

F32 = jnp.float32
BF16 = jnp.bfloat16
EPS = 1e-6
NEG_BIG = -1e30

V7X_VMEM_LIMIT_BYTES = 58 * 1024 * 1024
LANES = 128
SUBLANES = 8


def _params(*sem):
    return pltpu.CompilerParams(dimension_semantics=sem,
                                vmem_limit_bytes=V7X_VMEM_LIMIT_BYTES)


def _rms(x, g):
    ms = jnp.mean(x * x, axis=-1, keepdims=True)
    return x * lax.rsqrt(ms + EPS) * g


def _cast_kernel(w_ref, o_ref, *, scale):
    w = w_ref[...]
    if scale != 1.0:
        w = w * scale
    o_ref[...] = w.astype(o_ref.dtype)


def cast_bf16(w_stack, layer, *, scale=1.0, tr=256):
    _, k, n = w_stack.shape
    tr = min(tr, k)
    assert k % tr == 0
    return pl.pallas_call(
        functools.partial(_cast_kernel, scale=scale),
        grid=(k // tr,),
        in_specs=[pl.BlockSpec((None, tr, n), lambda i: (layer, i, 0))],
        out_specs=pl.BlockSpec((tr, n), lambda i: (i, 0)),
        out_shape=jax.ShapeDtypeStruct((k, n), BF16),
        compiler_params=_params("parallel"),
        name="cast_bf16",
    )(w_stack)

BF16_SUBLANE_TILE = 16


def _side_block(k, n, steps):
    if k % steps == 0 and (k // steps) % BF16_SUBLANE_TILE == 0:
        return (k // steps, n), lambda s: (s, 0)
    for s1 in (16, 8, 4, 2, 1):
        s2 = steps // s1
        if (steps % s1 == 0 and k % s1 == 0 and (k // s1) % BF16_SUBLANE_TILE == 0
                and n % s2 == 0 and (n // s2) % LANES == 0):
            return (k // s1, n // s2), lambda s: (s // s2, s % s2)
    raise ValueError(f"cannot tile a ({k}, {n}) weight in {steps} blocks")


def _pallas_with_side_casts(body, *, grid, in_specs, out_specs, out_shape, args, side_casts=(),
                            side_norm=None, **kwargs):
    out_specs, out_shape = list(out_specs), list(out_shape)
    n_in, n_out, n_side = len(in_specs), len(out_specs), len(side_casts)
    steps = math.prod(grid)
    strides = [math.prod(grid[a + 1:]) for a in range(len(grid))]

    def linear_step(*g):
        return sum(gi * st for gi, st in zip(g, strides))

    if side_norm is not None:
        nx, ngain, out_cols, col_block = side_norm
        nm, nk = nx.shape
        assert nm % steps == 0 and (nm // steps) % BF16_SUBLANE_TILE == 0
        norm_in_specs = [pl.BlockSpec((nm // steps, nk), lambda *g: (linear_step(*g), 0)),
                         pl.BlockSpec((1, nk), lambda *g: (0, 0))]
        norm_out_spec = pl.BlockSpec((nm // steps, nk), lambda *g: (linear_step(*g), col_block))
        norm_out_shape = jax.ShapeDtypeStruct((nm, out_cols), BF16)
        norm_args = [nx, ngain.reshape(1, nk).astype(F32)]
    scales = []
    for w_stack, layer, scale in side_casts:
        _, k, n = w_stack.shape
        blk, to_index = _side_block(k, n, steps)

        def index(*g, _to_index=to_index):
            return _to_index(linear_step(*g))

        in_specs = in_specs + [pl.BlockSpec((None,) + blk,
                                            lambda *g, _l=layer, _ix=index: (_l,) + _ix(*g))]
        out_specs.append(pl.BlockSpec(blk, index))
        out_shape.append(jax.ShapeDtypeStruct((k, n), BF16))
        args = list(args) + [w_stack]
        scales.append(scale)

    n_norm_in = 0
    if side_norm is not None:
        n_norm_in = len(norm_in_specs)
        in_specs = in_specs + norm_in_specs
        out_specs.append(norm_out_spec)
        out_shape.append(norm_out_shape)
        args = list(args) + norm_args
    n_all_in = n_in + n_side + n_norm_in
    n_side_out = n_side + (1 if side_norm is not None else 0)

    def kern(*refs):
        side_in = refs[n_in:n_in + n_side]
        outs = refs[n_all_in:n_all_in + n_out]
        side_out = refs[n_all_in + n_out:n_all_in + n_out + n_side_out]
        scratch = refs[n_all_in + n_out + n_side_out:]
        for si, so, scale in zip(side_in, side_out, scales):
            w = si[...]
            so[...] = (w if scale == 1.0 else w * scale).astype(so.dtype)
        if side_norm is not None:
            nx_ref, ng_ref = refs[n_in + n_side:n_all_in]
            side_out[-1][...] = _rms(nx_ref[...], ng_ref[...]).astype(BF16)
        body(*refs[:n_in], *outs, *scratch)

    res = pl.pallas_call(kern, grid=grid, in_specs=in_specs, out_specs=out_specs,
                         out_shape=out_shape, **kwargs)(*args)
    return tuple(res[:n_out]), tuple(res[n_out:])


def _rmsnorm_kernel(x_ref, g_ref, o_ref):
    o_ref[...] = _rms(x_ref[...].astype(F32), g_ref[...]).astype(o_ref.dtype)


def rmsnorm(x, g, out_dtype, tm=256):
    m, k = x.shape
    tm = min(tm, m)
    return pl.pallas_call(
        _rmsnorm_kernel,
        grid=(m // tm,),
        in_specs=[pl.BlockSpec((tm, k), lambda i: (i, 0)),
                  pl.BlockSpec((1, k), lambda i: (0, 0))],
        out_specs=pl.BlockSpec((tm, k), lambda i: (i, 0)),
        out_shape=jax.ShapeDtypeStruct((m, k), out_dtype),
        compiler_params=_params("parallel"),
        name="rmsnorm",
    )(x, g.reshape(1, k).astype(F32))


def _mm_kernel(*refs, has_res, norm_group, norm_lo, norm_hi, has_extra):
    x_ref, w_ref = refs[0], refs[1]
    pos = 2
    res_ref = g_ref = None
    if has_res:
        res_ref = refs[pos]
        pos += 1
    if norm_group:
        g_ref = refs[pos]
        pos += 1
    if has_extra:
        xw_ref, o_ref, xo_ref = refs[pos], refs[pos + 1], refs[pos + 2]

        @pl.when(pl.program_id(1) == 0)
        def _():
            xo_ref[...] = jnp.dot(x_ref[...], xw_ref[...], preferred_element_type=F32)
    else:
        o_ref = refs[pos]

    acc = jnp.dot(x_ref[...], w_ref[...], preferred_element_type=F32)
    if has_res:
        acc = acc + res_ref[...]
    if not norm_group:
        o_ref[...] = acc.astype(o_ref.dtype)
        return

    j = pl.program_id(1)
    in_range = jnp.logical_and(j >= norm_lo, j < norm_hi)

    @pl.when(in_range)
    def _():
        for c in range(acc.shape[1] // norm_group):
            sl = slice(c * norm_group, (c + 1) * norm_group)
            o_ref[:, sl] = _rms(acc[:, sl], g_ref[:, sl]).astype(o_ref.dtype)

    @pl.when(jnp.logical_not(in_range))
    def _():
        o_ref[...] = acc.astype(o_ref.dtype)


def matmul(x, w, out_dtype, *, tm=512, tn=512, n_cols=None, res=None,
           norm_gain=None, norm_group=0, norm_cols=None, side_casts=None, extra_w=None):
    m, k = x.shape
    n = w.shape[1] if n_cols is None else n_cols
    tm, tn = min(tm, m), min(tn, n)
    assert m % tm == 0 and n % tn == 0
    in_specs = [pl.BlockSpec((tm, k), lambda i, j: (i, 0)),
                pl.BlockSpec((k, tn), lambda i, j: (0, j))]
    args = [x, w]
    if res is not None:
        in_specs.append(pl.BlockSpec((tm, tn), lambda i, j: (i, j)))
        args.append(res)
    norm_lo = norm_hi = 0
    if norm_group:
        lo, hi = (0, n) if norm_cols is None else norm_cols
        assert tn % norm_group == 0 and lo % tn == 0 and hi % tn == 0
        norm_lo, norm_hi = lo // tn, hi // tn
        in_specs.append(pl.BlockSpec((1, tn), lambda i, j: (0, j)))
        args.append(norm_gain.reshape(1, n).astype(F32))
    out_specs = [pl.BlockSpec((tm, tn), lambda i, j: (i, j))]
    out_shape = [jax.ShapeDtypeStruct((m, n), out_dtype)]
    if extra_w is not None:
        assert side_casts is not None
        e = extra_w.shape[1]
        in_specs.append(pl.BlockSpec((k, e), lambda i, j: (0, 0)))
        args.append(extra_w)
        out_specs.append(pl.BlockSpec((tm, e), lambda i, j: (i, 0)))
        out_shape.append(jax.ShapeDtypeStruct((m, e), F32))
    outs, side = _pallas_with_side_casts(
        functools.partial(_mm_kernel, has_res=res is not None, norm_group=norm_group,
                          norm_lo=norm_lo, norm_hi=norm_hi, has_extra=extra_w is not None),
        grid=(m // tm, n // tn),
        in_specs=in_specs,
        out_specs=out_specs,
        out_shape=out_shape,
        args=args,
        side_casts=side_casts or (),
        compiler_params=_params("parallel", "arbitrary" if extra_w is not None else "parallel"),
        name="matmul",
    )
    return outs[0] if side_casts is None else (*outs, *side)


def _ffn_kernel(*refs, final_norm):
    if final_norm:
        x_ref, g_ref, wg_ref, wu_ref, wd_ref, fg_ref, o_ref, n_scr = refs
    else:
        x_ref, g_ref, wg_ref, wu_ref, wd_ref, o_ref, n_scr = refs
        fg_ref = None
    j = pl.program_id(1)

    @pl.when(j == 0)
    def _():
        x = x_ref[...]
        n_scr[...] = _rms(x, g_ref[...]).astype(BF16)
        o_ref[...] = x

    n = n_scr[...]
    gate = jnp.dot(n, wg_ref[...], preferred_element_type=F32)
    up = jnp.dot(n, wu_ref[...], preferred_element_type=F32)
    hid = (gate * jax.nn.sigmoid(gate) * up).astype(BF16)
    o_ref[...] += jnp.dot(hid, wd_ref[...], preferred_element_type=F32)

    if final_norm:
        @pl.when(j == pl.num_programs(1) - 1)
        def _():
            o_ref[...] = _rms(o_ref[...], fg_ref[...])


def ffn(x, g, wg, wu, wd_half, final_gain=None, *, tm=512, tf=256):
    m, d = x.shape
    dff = wg.shape[1]
    tm, tf = min(tm, m), min(tf, dff)
    assert m % tm == 0 and dff % tf == 0
    in_specs = [pl.BlockSpec((tm, d), lambda i, j: (i, 0)),
                pl.BlockSpec((1, d), lambda i, j: (0, 0)),
                pl.BlockSpec((d, tf), lambda i, j: (0, j)),
                pl.BlockSpec((d, tf), lambda i, j: (0, j)),
                pl.BlockSpec((tf, d), lambda i, j: (j, 0))]
    args = [x, g.reshape(1, d).astype(F32), wg, wu, wd_half]
    if final_gain is not None:
        in_specs.append(pl.BlockSpec((1, d), lambda i, j: (0, 0)))
        args.append(final_gain.reshape(1, d).astype(F32))
    return pl.pallas_call(
        functools.partial(_ffn_kernel, final_norm=final_gain is not None),
        grid=(m // tm, dff // tf),
        in_specs=in_specs,
        out_specs=pl.BlockSpec((tm, d), lambda i, j: (i, 0)),
        out_shape=jax.ShapeDtypeStruct((m, d), F32),
        scratch_shapes=[pltpu.VMEM((tm, d), BF16)],
        compiler_params=_params("parallel", "arbitrary"),
        name="ffn",
    )(*args)


def _s5_disc_kernel(are_ref, aim_ref, ls_ref, bre_ref, bim_ref,
                    abr_ref, abi_ref, bbr_ref, bbi_ref):
    ar, ai = are_ref[...], aim_ref[...]
    dt = jnp.exp(ls_ref[...])
    mag = jnp.exp(ar * dt)
    abar_r = mag * jnp.cos(ai * dt)
    abar_i = mag * jnp.sin(ai * dt)
    den = ar * ar + ai * ai
    pr, pi_ = abar_r - 1.0, abar_i
    coef_r = (pr * ar + pi_ * ai) / den
    coef_i = (pi_ * ar - pr * ai) / den
    br, bi = bre_ref[...], bim_ref[...]
    abr_ref[...] = abar_r
    abi_ref[...] = abar_i
    bbr_ref[...] = coef_r * br - coef_i * bi
    bbi_ref[...] = coef_r * bi + coef_i * br


def s5_discretise(a_re, a_im, log_step, b_re, b_im):
    g, n = a_re.shape
    p = b_re.shape[-1]
    col = jax.ShapeDtypeStruct((g, n, 1), F32)
    mat = jax.ShapeDtypeStruct((g, n, p), F32)
    ls = jnp.broadcast_to(log_step.reshape(g, 1, 1), (g, n, 1))
    abr, abi, bbr, bbi = pl.pallas_call(
        _s5_disc_kernel,
        out_shape=(col, col, mat, mat),
        compiler_params=pltpu.CompilerParams(vmem_limit_bytes=V7X_VMEM_LIMIT_BYTES),
        name="s5_discretise",
    )(a_re.reshape(g, n, 1), a_im.reshape(g, n, 1), ls, b_re, b_im)
    return abr.reshape(g, n), abi.reshape(g, n), bbr, bbi


GROUPS_PER_HALF = 8


def _gelu_tanh(x):
    c = math.sqrt(2.0 / math.pi)
    return 0.5 * x * (1.0 + jnp.tanh(c * (x + 0.044715 * (x * x * x))))


def _s5_kernel(u_ref, rb_ref, rc_ref, ar_ref, ai_ref, d_ref, y_ref,
               lhs_scr, sr_scr, si_scr, o2_scr, xr_scr, xi_scr, *, nb, hs, tt, sw, hw):
    @pl.when(pl.program_id(1) == 0)
    def _():
        xr_scr[...] = jnp.zeros_like(xr_scr)
        xi_scr[...] = jnp.zeros_like(xi_scr)

    zeros = jnp.zeros((tt, hw), F32)
    for b in range(nb):
        ub = u_ref[b].astype(F32)
        for hh in range(hs):
            rows = pl.ds(hh * nb + b, tt, stride=SUBLANES)
            for c in range(hs):
                lhs_scr[c, rows, :] = ub[:, c * hw:(c + 1) * hw] if c == hh else zeros

    lhs = jnp.concatenate([lhs_scr[c] for c in range(hs)], axis=-1).astype(BF16)
    hm = lhs.shape[0] // 2
    for r0 in (0, hm):
        sr_scr[r0:r0 + hm, :] = jnp.dot(lhs[r0:r0 + hm], rb_ref[0, :, :sw],
                                        preferred_element_type=F32)
        si_scr[r0:r0 + hm, :] = jnp.dot(lhs[r0:r0 + hm], rb_ref[0, :, sw:],
                                        preferred_element_type=F32)

    ar, ai = ar_ref[0], ai_ref[0]

    def step(t, carry):
        xr, xi = carry
        rows = pl.ds(pl.multiple_of(t * SUBLANES, SUBLANES), SUBLANES)
        nxr = ar * xr - ai * xi + sr_scr[rows, :]
        nxi = ar * xi + ai * xr + si_scr[rows, :]
        sr_scr[rows, :] = nxr
        si_scr[rows, :] = nxi
        return nxr, nxi

    xr, xi = lax.fori_loop(0, tt, step, (xr_scr[...], xi_scr[...]), unroll=8)
    xr_scr[...] = xr
    xi_scr[...] = xi

    for r0 in (0, hm):
        o2 = (jnp.dot(sr_scr[r0:r0 + hm, :].astype(BF16), rc_ref[0, :sw, :],
                      preferred_element_type=F32)
              + jnp.dot(si_scr[r0:r0 + hm, :].astype(BF16), rc_ref[0, sw:, :],
                        preferred_element_type=F32))
        for c in range(hs):
            o2_scr[c, r0:r0 + hm, :] = o2[:, c * hw:(c + 1) * hw]

    for b in range(nb):
        parts = [o2_scr[hh, pl.ds(hh * nb + b, tt, stride=SUBLANES), :] for hh in range(hs)]
        yb = jnp.concatenate(parts, axis=-1) + d_ref[...] * u_ref[b].astype(F32)
        y_ref[b] = _gelu_tanh(yb).astype(y_ref.dtype)


def s5_core(z3, abar_r, abar_i, bbar_r, bbar_i, c_re, c_im, d_skip, *, tt=256, side_casts=None,
            side_norm=None):
    nb, seq, _ = z3.shape
    g, n, p = bbar_r.shape
    assert SUBLANES % nb == 0
    hs = SUBLANES // nb
    gph = GROUPS_PER_HALF
    assert g % (hs * gph) == 0
    nblk = g // (hs * gph)
    hw, sw = gph * p, gph * n
    cw = hs * hw
    assert hw == LANES and sw % LANES == 0
    tt = min(tt, seq)
    assert seq % tt == 0

    eye = jnp.eye(gph, dtype=F32)

    def in_proj(bb):
        bb = bb.reshape(nblk, hs, gph, n, p)
        return jnp.einsum("jhqnp,qr->jhqprn", bb, eye).reshape(nblk, cw, sw)

    def out_proj(cc):
        cc = cc.reshape(nblk, hs, gph, p, n)
        return jnp.einsum("jhrpn,qr->jqnhrp", cc, eye).reshape(nblk, sw, cw)

    rb = jnp.concatenate([in_proj(bbar_r), in_proj(bbar_i)], axis=-1).astype(BF16)
    rc = jnp.concatenate([out_proj(c_re), out_proj(-c_im)], axis=1).astype(BF16)

    def rows(a):
        a = a.reshape(nblk, hs, 1, sw)
        return jnp.broadcast_to(a, (nblk, hs, nb, sw)).reshape(nblk, hs * nb, sw)

    (y,), side = _pallas_with_side_casts(
        functools.partial(_s5_kernel, nb=nb, hs=hs, tt=tt, sw=sw, hw=hw),
        grid=(nblk, seq // tt),
        in_specs=[pl.BlockSpec((nb, tt, cw), lambda j, t: (0, t, j)),
                  pl.BlockSpec((1, cw, 2 * sw), lambda j, t: (j, 0, 0)),
                  pl.BlockSpec((1, 2 * sw, cw), lambda j, t: (j, 0, 0)),
                  pl.BlockSpec((1, SUBLANES, sw), lambda j, t: (j, 0, 0)),
                  pl.BlockSpec((1, SUBLANES, sw), lambda j, t: (j, 0, 0)),
                  pl.BlockSpec((1, cw), lambda j, t: (0, j))],
        out_specs=[pl.BlockSpec((nb, tt, cw), lambda j, t: (0, t, j))],
        out_shape=[jax.ShapeDtypeStruct((nb, seq, g * p), BF16)],
        args=[z3, rb, rc, rows(abar_r), rows(abar_i), d_skip.reshape(1, g * p).astype(F32)],
        side_casts=side_casts or (),
        side_norm=side_norm,
        scratch_shapes=[pltpu.VMEM((hs, SUBLANES * tt, hw), F32),
                        pltpu.VMEM((SUBLANES * tt, sw), F32),
                        pltpu.VMEM((SUBLANES * tt, sw), F32),
                        pltpu.VMEM((hs, SUBLANES * tt, hw), F32),
                        pltpu.VMEM((SUBLANES, sw), F32),
                        pltpu.VMEM((SUBLANES, sw), F32)],
        compiler_params=_params("parallel", "arbitrary"),
        name="s5_core",
    )
    return y if side_casts is None and side_norm is None else (y, *side)


def _glu_kernel(y_ref, w_ref, g_ref, into_ref, o_ref):
    del into_ref
    y = y_ref[...]
    v = jnp.dot(y, w_ref[...], preferred_element_type=F32)
    o = y.astype(F32) * jax.nn.sigmoid(v)
    o_ref[...] = _rms(o, g_ref[...]).astype(o_ref.dtype)


def glu_norm(y, w, g, *, into, tm=512):
    m, k = y.shape
    tm = min(tm, m)
    assert into.shape[0] == m and into.shape[1] % k == 0 and into.dtype == BF16
    return pl.pallas_call(
        _glu_kernel,
        grid=(m // tm,),
        in_specs=[pl.BlockSpec((tm, k), lambda i: (i, 0)),
                  pl.BlockSpec((k, k), lambda i: (0, 0)),
                  pl.BlockSpec((1, k), lambda i: (0, 0)),
                  pl.BlockSpec(memory_space=pl.ANY)],
        out_specs=pl.BlockSpec((tm, k), lambda i: (i, 0)),
        out_shape=jax.ShapeDtypeStruct(into.shape, BF16),
        input_output_aliases={3: 0},
        compiler_params=_params("parallel"),
        name="glu_norm",
    )(y, w, g.reshape(1, k).astype(F32), into)


LOG2E = math.log2(math.e)
def _cumsum_kernel(f_ref, bf_ref, c_ref, *, chunk):
    seq = f_ref.shape[1]
    row = lax.broadcasted_iota(jnp.int32, (chunk, chunk), 0)
    col = lax.broadcasted_iota(jnp.int32, (chunk, chunk), 1)
    tri = (col <= row).astype(F32)
    carry = jnp.zeros((1, f_ref.shape[2]), F32)
    for ci in range(seq // chunk):
        sl = slice(ci * chunk, (ci + 1) * chunk)
        x = f_ref[0, sl, :] + bf_ref[...]
        log_f = (jnp.minimum(x, 0.0) - jnp.log1p(jnp.exp(-jnp.abs(x)))) * LOG2E
        cs = jnp.dot(tri, log_f, precision=lax.Precision.HIGHEST,
                     preferred_element_type=F32) + carry
        c_ref[0, sl, :] = cs
        carry = cs[chunk - 1:chunk, :]


def forget_cumsum(f3, b_f_row, *, chunk=256):
    nb, seq, w = f3.shape
    chunk = min(chunk, seq)
    return pl.pallas_call(
        functools.partial(_cumsum_kernel, chunk=chunk),
        grid=(nb,),
        in_specs=[pl.BlockSpec((1, seq, w), lambda b: (b, 0, 0)),
                  pl.BlockSpec((1, w), lambda b: (0, 0))],
        out_specs=pl.BlockSpec((1, seq, w), lambda b: (b, 0, 0)),
        out_shape=jax.ShapeDtypeStruct((nb, seq, w), F32),
        compiler_params=_params("parallel"),
        name="forget_cumsum",
    )(f3, b_f_row)


FOX_HEADS_PER_STEP = 2


def _fox_kernel(q_ref, k_ref, v_ref, cq_ref, ck_ref, o_ref, *, tq, big, nh, dh):
    i = pl.program_id(2)
    per_big = big // tq

    def scores(hd, start, width, diagonal):
        cols = slice(hd * dh, (hd + 1) * dh)
        rows = pl.ds(start, width)
        s = lax.dot_general(q_ref[0, :, cols], k_ref[0, rows, cols], (((1,), (1,)), ((), ())),
                            preferred_element_type=F32)
        s = s + (cq_ref[0, hd] - ck_ref[0, hd, :, rows])
        if diagonal:
            r = lax.broadcasted_iota(jnp.int32, s.shape, 0)
            c = lax.broadcasted_iota(jnp.int32, s.shape, 1)
            s = jnp.where(c <= r, s, NEG_BIG)
        return s

    def update(hd, s, start, width, carry):
        m, l, acc = carry
        v = v_ref[0, pl.ds(start, width), hd * dh:(hd + 1) * dh]
        m_new = jnp.maximum(m, jnp.max(s, axis=-1, keepdims=True))
        alpha = jnp.exp2(m - m_new)
        p = jnp.exp2(s - m_new)
        l = alpha * l + jnp.sum(p, axis=-1, keepdims=True)
        acc = alpha * acc + jnp.dot(p.astype(v.dtype), v, preferred_element_type=F32)
        return m_new, l, acc

    def block(start, width, carries, diagonal):
        ss = [scores(hd, start, width, diagonal) for hd in range(nh)]
        return tuple(update(hd, ss[hd], start, width, carries[hd]) for hd in range(nh))

    init = tuple((jnp.full((tq, 1), NEG_BIG, F32), jnp.zeros((tq, 1), F32),
                  jnp.zeros((tq, dh), F32)) for _ in range(nh))
    n_big = i // per_big
    carries = lax.fori_loop(
        0, n_big, lambda j, c: block(pl.multiple_of(j * big, big), big, c, False), init)
    carries = lax.fori_loop(
        n_big * per_big, i, lambda j, c: block(pl.multiple_of(j * tq, tq), tq, c, False), carries)
    carries = block(pl.multiple_of(i * tq, tq), tq, carries, True)
    for hd in range(nh):
        _, l, acc = carries[hd]
        o_ref[0, :, hd * dh:(hd + 1) * dh] = (acc / l).astype(o_ref.dtype)


def fox_attention(z3, c, *, q_off, k_off, v_off, heads, dh, tq=1024, big=1024, side_casts=None):
    nb, seq, _ = z3.shape
    tq = min(tq, seq)
    big = min(big, seq)
    nq = seq // tq
    nh = FOX_HEADS_PER_STEP if heads % FOX_HEADS_PER_STEP == 0 else 1
    w = nh * dh
    assert seq % tq == 0 and big % tq == 0
    assert q_off % w == 0 and k_off % w == 0 and v_off % w == 0
    ch = jnp.transpose(c[:, :, :heads], (0, 2, 1))
    cq = ch.reshape(nb, heads, seq, 1)
    ck = ch.reshape(nb, heads, 1, seq)
    qb, kb, vb = q_off // w, k_off // w, v_off // w
    (out,), side = _pallas_with_side_casts(
        functools.partial(_fox_kernel, tq=tq, big=big, nh=nh, dh=dh),
        grid=(nb, heads // nh, nq),
        in_specs=[pl.BlockSpec((1, tq, w), lambda b, h, i: (b, i, qb + h)),
                  pl.BlockSpec((1, seq, w), lambda b, h, i: (b, 0, kb + h)),
                  pl.BlockSpec((1, seq, w), lambda b, h, i: (b, 0, vb + h)),
                  pl.BlockSpec((1, nh, tq, 1), lambda b, h, i: (b, h, i, 0)),
                  pl.BlockSpec((1, nh, 1, seq), lambda b, h, i: (b, h, 0, 0))],
        out_specs=[pl.BlockSpec((1, tq, w), lambda b, h, i: (b, i, h))],
        out_shape=[jax.ShapeDtypeStruct((nb, seq, heads * dh), F32)],
        args=[z3, z3, z3, cq, ck],
        side_casts=side_casts or (),
        compiler_params=_params("parallel", "parallel", "parallel"),
        name="fox_attention",
    )
    return out if side_casts is None else (out, *side)


def _xattn_kernel(q_ref, k_ref, v_ref, o_ref, *, heads, dh):
    scores = [lax.dot_general(q_ref[:, hd * dh:(hd + 1) * dh], k_ref[:, hd * dh:(hd + 1) * dh],
                              (((1,), (1,)), ((), ())), preferred_element_type=F32)
              for hd in range(heads)]
    for hd in range(heads):
        sl = slice(hd * dh, (hd + 1) * dh)
        s = scores[hd]
        s = s - jnp.max(s, axis=-1, keepdims=True)
        p = jnp.exp(s)
        p = p / jnp.sum(p, axis=-1, keepdims=True)
        o_ref[:, sl] = jnp.dot(p.astype(BF16), v_ref[:, sl],
                               preferred_element_type=F32).astype(o_ref.dtype)


def cross_attention(q, k, v, *, nb, heads, tm=512):
    t, w = q.shape
    seq, mlen = t // nb, k.shape[0] // nb
    tm = min(tm, seq)
    nt = seq // tm
    return pl.pallas_call(
        functools.partial(_xattn_kernel, heads=heads, dh=w // heads),
        grid=(nb, nt),
        in_specs=[pl.BlockSpec((tm, w), lambda b, i: (b * nt + i, 0)),
                  pl.BlockSpec((mlen, w), lambda b, i: (b, 0)),
                  pl.BlockSpec((mlen, w), lambda b, i: (b, 0))],
        out_specs=pl.BlockSpec((tm, w), lambda b, i: (b * nt + i, 0)),
        out_shape=jax.ShapeDtypeStruct((t, w), BF16),
        compiler_params=_params("parallel", "parallel"),
        name="cross_attention",
    )(q, k, v)


def kernel(x, mem, ffn1_norm, ffn1_w_gate, ffn1_w_up, ffn1_w_down, mix_norm, w_in, ssm_A_re, ssm_A_im, ssm_log_step, ssm_B_re, ssm_B_im, ssm_C_re, ssm_C_im, ssm_D, ssm_w_glu, ssm_out_norm, fox_b_f, fox_q_norm, fox_k_norm, fox_out_norm, w_out, xattn_norm, mem_norm, xattn_wq, xattn_wk, xattn_wv, xattn_q_norm, xattn_k_norm, xattn_wo, ffn2_norm, ffn2_w_gate, ffn2_w_up, ffn2_w_down, final_norm):
    nb, seq, d = x.shape
    t = nb * seq
    depth = ffn1_norm.shape[0]
    g, n_state = ssm_A_re.shape[1:]
    p = ssm_B_re.shape[-1]
    w_ssm = g * p
    fox_h, fox_dh = fox_b_f.shape[-1], fox_q_norm.shape[-1]
    w_fox = fox_h * fox_dh
    o_q, o_k, o_v, o_f = w_ssm, w_ssm + w_fox, w_ssm + 2 * w_fox, w_ssm + 3 * w_fox
    x_dh = xattn_q_norm.shape[-1]
    x_w = xattn_wq.shape[-1]
    x_heads = x_w // x_dh
    mlen = mem.shape[1]
    tn_in = min(1024, w_ssm, w_fox)
    assert w_ssm % w_fox == 0

    h = x.reshape(t, d)
    mem2 = mem.reshape(nb * mlen, d)

    ffn1_w = (cast_bf16(ffn1_w_gate, 0), cast_bf16(ffn1_w_up, 0),
              cast_bf16(ffn1_w_down, 0, scale=0.5))
    w_in_b = cast_bf16(w_in, 0)

    for l in range(depth):
        more = l + 1 < depth
        h = ffn(h, ffn1_norm[l], *ffn1_w)

        hn = rmsnorm(h, mix_norm[l], BF16)
        gain_in = jnp.concatenate([
            jnp.ones((w_ssm,), F32),
            jnp.tile(fox_q_norm[l].astype(F32) * (fox_dh ** -0.5 * LOG2E), fox_h),
            jnp.tile(fox_k_norm[l].astype(F32), fox_h),
            jnp.ones((w_fox,), F32)])
        w_f = jnp.pad(w_in_b[:, o_f:], ((0, 0), (0, LANES - fox_h)))
        z, f_logit, w2_gate, w2_up = matmul(
            hn, w_in_b, BF16, tm=1024, tn=tn_in, n_cols=o_f, norm_gain=gain_in,
            norm_group=fox_dh, norm_cols=(o_q, o_v), extra_w=w_f,
            side_casts=[(ffn2_w_gate, l, 1.0), (ffn2_w_up, l, 1.0)])
        z3 = z.reshape(nb, seq, o_f)

        b_f_row = jnp.pad(fox_b_f[l].astype(F32), (0, LANES - fox_h)).reshape(1, LANES)
        c = forget_cumsum(f_logit.reshape(nb, seq, LANES), b_f_row)
        y_fox, wv_b, wo_b, w2_down, *next_rest = fox_attention(
            z3, c, q_off=o_q, k_off=o_k, v_off=o_v, heads=fox_h, dh=fox_dh,
            side_casts=[(xattn_wv, l, 1.0), (xattn_wo, l, 1.0), (ffn2_w_down, l, 0.5)]
            + ([(ffn1_w_down, l + 1, 0.5), (w_in, l + 1, 1.0)] if more else []))
        ffn2_w = (w2_gate, w2_up, w2_down)

        abar_r, abar_i, bbar_r, bbar_i = s5_discretise(
            ssm_A_re[l], ssm_A_im[l], ssm_log_step[l], ssm_B_re[l], ssm_B_im[l])
        y_ssm, w_glu_b, w_out_b, wq_b, wk_b, *next_gate_up, y = s5_core(
            z3, abar_r, abar_i, bbar_r, bbar_i,
            ssm_C_re[l].astype(F32), ssm_C_im[l].astype(F32), ssm_D[l],
            side_casts=[(ssm_w_glu, l, 1.0), (w_out, l, 1.0), (xattn_wq, l, 1.0),
                        (xattn_wk, l, 1.0)]
            + ([(ffn1_w_gate, l + 1, 1.0), (ffn1_w_up, l + 1, 1.0)] if more else []),
            side_norm=(y_fox.reshape(t, w_fox), fox_out_norm[l], w_ssm + w_fox, w_ssm // w_fox))
        y = glu_norm(y_ssm.reshape(t, w_ssm), w_glu_b, ssm_out_norm[l], into=y)
        if more:
            ffn1_w = (*next_gate_up, next_rest[0])
            w_in_b = next_rest[1]

        h = matmul(y, w_out_b, F32, tm=1024, tn=1024, res=h)

        hn = rmsnorm(h, xattn_norm[l], BF16)
        mn = rmsnorm(mem2, mem_norm[l], BF16)
        q = matmul(hn, wq_b, BF16, tm=1024, tn=x_dh,
                   norm_gain=jnp.tile(xattn_q_norm[l].astype(F32) * (x_dh ** -0.5), x_heads),
                   norm_group=x_dh)
        k = matmul(mn, wk_b, BF16, tm=1024, tn=x_dh,
                   norm_gain=jnp.tile(xattn_k_norm[l].astype(F32), x_heads), norm_group=x_dh)
        v = matmul(mn, wv_b, BF16, tm=1024, tn=x_dh)
        o = cross_attention(q, k, v, nb=nb, heads=x_heads)
        h = matmul(o, wo_b, F32, tm=1024, tn=1024, res=h)

        h = ffn(h, ffn2_norm[l], *ffn2_w, final_gain=final_norm[l])

    return h.reshape(nb, seq, d)
```

```python
import functools
import math

import jax
import jax.numpy as jnp
from jax import lax
from jax.experimental import pallas as pl
from jax.experimental.pallas import tpu as pltpu

F32 = jnp.float32
BF16 = jnp.bfloat16
EPS = 1e-6
NEG_BIG = -1e30

V7X_VMEM_LIMIT_BYTES = 58 * 1024 * 1024
LANES = 128
SUBLANES = 8


def _params(*sem):
    return pltpu.CompilerParams(dimension_semantics=sem,
                                vmem_limit_bytes=V7X_VMEM_LIMIT_BYTES)


def _rms(x, g):
    ms = jnp.mean(x * x, axis=-1, keepdims=True)
    return x * lax.rsqrt(ms + EPS) * g


def _cast_kernel(w_ref, o_ref, *, scale):
    w = w_ref[...]
    if scale != 1.0:
        w = w * scale
    o_ref[...] = w.astype(o_ref.dtype)


def cast_bf16(w_stack, layer, *, scale=1.0, tr=256):
    _, k, n = w_stack.shape
    tr = min(tr, k)
    assert k % tr == 0
    return pl.pallas_call(
        functools.partial(_cast_kernel, scale=scale),
        grid=(k // tr,),
        in_specs=[pl.BlockSpec((None, tr, n), lambda i: (layer, i, 0))],
        out_specs=pl.BlockSpec((tr, n), lambda i: (i, 0)),
        out_shape=jax.ShapeDtypeStruct((k, n), BF16),
        compiler_params=_params("parallel"),
        name="cast_bf16",
    )(w_stack)

BF16_SUBLANE_TILE = 16


def _side_block(k, n, steps):
    if k % steps == 0 and (k // steps) % BF16_SUBLANE_TILE == 0:
        return (k // steps, n), lambda s: (s, 0)
    for s1 in (16, 8, 4, 2, 1):
        s2 = steps // s1
        if (steps % s1 == 0 and k % s1 == 0 and (k // s1) % BF16_SUBLANE_TILE == 0
                and n % s2 == 0 and (n // s2) % LANES == 0):
            return (k // s1, n // s2), lambda s: (s // s2, s % s2)
    raise ValueError(f"cannot tile a ({k}, {n}) weight in {steps} blocks")


def _pallas_with_side_casts(body, *, grid, in_specs, out_specs, out_shape, args, side_casts=(),
                            side_norm=None, **kwargs):
    out_specs, out_shape = list(out_specs), list(out_shape)
    n_in, n_out, n_side = len(in_specs), len(out_specs), len(side_casts)
    steps = math.prod(grid)
    strides = [math.prod(grid[a + 1:]) for a in range(len(grid))]

    def linear_step(*g):
        return sum(gi * st for gi, st in zip(g, strides))

    if side_norm is not None:
        nx, ngain, out_cols, col_block = side_norm
        nm, nk = nx.shape
        assert nm % steps == 0 and (nm // steps) % BF16_SUBLANE_TILE == 0
        norm_in_specs = [pl.BlockSpec((nm // steps, nk), lambda *g: (linear_step(*g), 0)),
                         pl.BlockSpec((1, nk), lambda *g: (0, 0))]
        norm_out_spec = pl.BlockSpec((nm // steps, nk), lambda *g: (linear_step(*g), col_block))
        norm_out_shape = jax.ShapeDtypeStruct((nm, out_cols), BF16)
        norm_args = [nx, ngain.reshape(1, nk).astype(F32)]
    scales = []
    for w_stack, layer, scale in side_casts:
        _, k, n = w_stack.shape
        blk, to_index = _side_block(k, n, steps)

        def index(*g, _to_index=to_index):
            return _to_index(linear_step(*g))

        in_specs = in_specs + [pl.BlockSpec((None,) + blk,
                                            lambda *g, _l=layer, _ix=index: (_l,) + _ix(*g))]
        out_specs.append(pl.BlockSpec(blk, index))
        out_shape.append(jax.ShapeDtypeStruct((k, n), BF16))
        args = list(args) + [w_stack]
        scales.append(scale)

    n_norm_in = 0
    if side_norm is not None:
        n_norm_in = len(norm_in_specs)
        in_specs = in_specs + norm_in_specs
        out_specs.append(norm_out_spec)
        out_shape.append(norm_out_shape)
        args = list(args) + norm_args
    n_all_in = n_in + n_side + n_norm_in
    n_side_out = n_side + (1 if side_norm is not None else 0)

    def kern(*refs):
        side_in = refs[n_in:n_in + n_side]
        outs = refs[n_all_in:n_all_in + n_out]
        side_out = refs[n_all_in + n_out:n_all_in + n_out + n_side_out]
        scratch = refs[n_all_in + n_out + n_side_out:]
        for si, so, scale in zip(side_in, side_out, scales):
            w = si[...]
            so[...] = (w if scale == 1.0 else w * scale).astype(so.dtype)
        if side_norm is not None:
            nx_ref, ng_ref = refs[n_in + n_side:n_all_in]
            side_out[-1][...] = _rms(nx_ref[...], ng_ref[...]).astype(BF16)
        body(*refs[:n_in], *outs, *scratch)

    res = pl.pallas_call(kern, grid=grid, in_specs=in_specs, out_specs=out_specs,
                         out_shape=out_shape, **kwargs)(*args)
    return tuple(res[:n_out]), tuple(res[n_out:])


def _rmsnorm_kernel(x_ref, g_ref, o_ref):
    o_ref[...] = _rms(x_ref[...].astype(F32), g_ref[...]).astype(o_ref.dtype)


def rmsnorm(x, g, out_dtype, tm=256):
    m, k = x.shape
    tm = min(tm, m)
    return pl.pallas_call(
        _rmsnorm_kernel,
        grid=(m // tm,),
        in_specs=[pl.BlockSpec((tm, k), lambda i: (i, 0)),
                  pl.BlockSpec((1, k), lambda i: (0, 0))],
        out_specs=pl.BlockSpec((tm, k), lambda i: (i, 0)),
        out_shape=jax.ShapeDtypeStruct((m, k), out_dtype),
        compiler_params=_params("parallel"),
        name="rmsnorm",
    )(x, g.reshape(1, k).astype(F32))


NORM_ROW_CHUNKS = 4


def _mm_kernel(*refs, has_res, norm_group, has_extra):
    x_ref, w_ref = refs[0], refs[1]
    pos = 2
    res_ref = g_ref = None
    if has_res:
        res_ref = refs[pos]
        pos += 1
    if norm_group:
        g_ref = refs[pos]
        pos += 1
    if has_extra:
        xw_ref, o_ref, xo_ref = refs[pos], refs[pos + 1], refs[pos + 2]

        @pl.when(pl.program_id(1) == 0)
        def _():
            xo_ref[...] = jnp.dot(x_ref[...], xw_ref[...], preferred_element_type=F32)
    else:
        o_ref = refs[pos]

    if not norm_group:
        acc = jnp.dot(x_ref[...], w_ref[...], preferred_element_type=F32)
        if has_res:
            acc = acc + res_ref[...]
        o_ref[...] = acc.astype(o_ref.dtype)
        return

    tm = x_ref.shape[0]
    chunks = NORM_ROW_CHUNKS if tm % (NORM_ROW_CHUNKS * BF16_SUBLANE_TILE) == 0 else 1
    ch = tm // chunks
    accs = [jnp.dot(x_ref[r * ch:(r + 1) * ch, :], w_ref[...], preferred_element_type=F32)
            for r in range(chunks)]
    for r, acc in enumerate(accs):
        rows = slice(r * ch, (r + 1) * ch)
        if has_res:
            acc = acc + res_ref[rows, :]
        for c in range(acc.shape[1] // norm_group):
            sl = slice(c * norm_group, (c + 1) * norm_group)
            o_ref[rows, sl] = _rms(acc[:, sl], g_ref[:, sl]).astype(o_ref.dtype)


def matmul(x, w, out_dtype, *, tm=512, tn=512, n_cols=None, w_col_tile=None, res=None,
           norm_gain=None, norm_group=0, side_casts=None, extra_w=None):
    m, k = x.shape
    n = w.shape[1] if n_cols is None else n_cols
    tm, tn = min(tm, m), min(tn, n)
    assert m % tm == 0 and n % tn == 0
    if w_col_tile is None:
        w_col_tile = lambda j: j
    in_specs = [pl.BlockSpec((tm, k), lambda i, j: (i, 0)),
                pl.BlockSpec((k, tn), lambda i, j: (0, w_col_tile(j)))]
    args = [x, w]
    if res is not None:
        in_specs.append(pl.BlockSpec((tm, tn), lambda i, j: (i, j)))
        args.append(res)
    if norm_group:
        assert tn % norm_group == 0
        in_specs.append(pl.BlockSpec((1, tn), lambda i, j: (0, j)))
        args.append(norm_gain.reshape(1, n).astype(F32))
    out_specs = [pl.BlockSpec((tm, tn), lambda i, j: (i, j))]
    out_shape = [jax.ShapeDtypeStruct((m, n), out_dtype)]
    if extra_w is not None:
        assert side_casts is not None
        e = extra_w.shape[1]
        in_specs.append(pl.BlockSpec((k, e), lambda i, j: (0, 0)))
        args.append(extra_w)
        out_specs.append(pl.BlockSpec((tm, e), lambda i, j: (i, 0)))
        out_shape.append(jax.ShapeDtypeStruct((m, e), F32))
    outs, side = _pallas_with_side_casts(
        functools.partial(_mm_kernel, has_res=res is not None, norm_group=norm_group,
                          has_extra=extra_w is not None),
        grid=(m // tm, n // tn),
        in_specs=in_specs,
        out_specs=out_specs,
        out_shape=out_shape,
        args=args,
        side_casts=side_casts or (),
        compiler_params=_params("parallel", "arbitrary" if extra_w is not None else "parallel"),
        name="matmul",
    )
    return outs[0] if side_casts is None else (*outs, *side)


def _ffn_kernel(*refs, final_norm):
    if final_norm:
        x_ref, g_ref, wg_ref, wu_ref, wd_ref, fg_ref, o_ref, n_scr = refs
    else:
        x_ref, g_ref, wg_ref, wu_ref, wd_ref, o_ref, n_scr = refs
        fg_ref = None
    j = pl.program_id(1)

    @pl.when(j == 0)
    def _():
        x = x_ref[...]
        n_scr[...] = _rms(x, g_ref[...]).astype(BF16)
        o_ref[...] = x

    n = n_scr[...]
    gate = jnp.dot(n, wg_ref[...], preferred_element_type=F32)
    up = jnp.dot(n, wu_ref[...], preferred_element_type=F32)
    hid = (gate * jax.nn.sigmoid(gate) * up).astype(BF16)
    o_ref[...] += jnp.dot(hid, wd_ref[...], preferred_element_type=F32)

    if final_norm:
        @pl.when(j == pl.num_programs(1) - 1)
        def _():
            o_ref[...] = _rms(o_ref[...], fg_ref[...])


def ffn(x, g, wg, wu, wd_half, final_gain=None, *, tm=512, tf=256):
    m, d = x.shape
    dff = wg.shape[1]
    tm, tf = min(tm, m), min(tf, dff)
    assert m % tm == 0 and dff % tf == 0
    in_specs = [pl.BlockSpec((tm, d), lambda i, j: (i, 0)),
                pl.BlockSpec((1, d), lambda i, j: (0, 0)),
                pl.BlockSpec((d, tf), lambda i, j: (0, j)),
                pl.BlockSpec((d, tf), lambda i, j: (0, j)),
                pl.BlockSpec((tf, d), lambda i, j: (j, 0))]
    args = [x, g.reshape(1, d).astype(F32), wg, wu, wd_half]
    if final_gain is not None:
        in_specs.append(pl.BlockSpec((1, d), lambda i, j: (0, 0)))
        args.append(final_gain.reshape(1, d).astype(F32))
    return pl.pallas_call(
        functools.partial(_ffn_kernel, final_norm=final_gain is not None),
        grid=(m // tm, dff // tf),
        in_specs=in_specs,
        out_specs=pl.BlockSpec((tm, d), lambda i, j: (i, 0)),
        out_shape=jax.ShapeDtypeStruct((m, d), F32),
        scratch_shapes=[pltpu.VMEM((tm, d), BF16)],
        compiler_params=_params("parallel", "arbitrary"),
        name="ffn",
    )(*args)


def _s5_disc_kernel(are_ref, aim_ref, ls_ref, bre_ref, bim_ref,
                    abr_ref, abi_ref, bbr_ref, bbi_ref):
    ar, ai = are_ref[...], aim_ref[...]
    dt = jnp.exp(ls_ref[...])
    mag = jnp.exp(ar * dt)
    abar_r = mag * jnp.cos(ai * dt)
    abar_i = mag * jnp.sin(ai * dt)
    den = ar * ar + ai * ai
    pr, pi_ = abar_r - 1.0, abar_i
    coef_r = (pr * ar + pi_ * ai) / den
    coef_i = (pi_ * ar - pr * ai) / den
    br, bi = bre_ref[...], bim_ref[...]
    abr_ref[...] = abar_r
    abi_ref[...] = abar_i
    bbr_ref[...] = coef_r * br - coef_i * bi
    bbi_ref[...] = coef_r * bi + coef_i * br


def s5_discretise(a_re, a_im, log_step, b_re, b_im):
    g, n = a_re.shape
    p = b_re.shape[-1]
    col = jax.ShapeDtypeStruct((g, n, 1), F32)
    mat = jax.ShapeDtypeStruct((g, n, p), F32)
    ls = jnp.broadcast_to(log_step.reshape(g, 1, 1), (g, n, 1))
    abr, abi, bbr, bbi = pl.pallas_call(
        _s5_disc_kernel,
        out_shape=(col, col, mat, mat),
        compiler_params=pltpu.CompilerParams(vmem_limit_bytes=V7X_VMEM_LIMIT_BYTES),
        name="s5_discretise",
    )(a_re.reshape(g, n, 1), a_im.reshape(g, n, 1), ls, b_re, b_im)
    return abr.reshape(g, n), abi.reshape(g, n), bbr, bbi


GROUPS_PER_HALF = 8


def _gelu_tanh(x):
    c = math.sqrt(2.0 / math.pi)
    return 0.5 * x * (1.0 + jnp.tanh(c * (x + 0.044715 * (x * x * x))))


def _s5_kernel(u_ref, rb_ref, rc_ref, ar_ref, ai_ref, d_ref, y_ref,
               lhs_scr, sr_scr, si_scr, o2_scr, xr_scr, xi_scr, *, nb, hs, tt, sw, hw):
    @pl.when(pl.program_id(1) == 0)
    def _():
        xr_scr[...] = jnp.zeros_like(xr_scr)
        xi_scr[...] = jnp.zeros_like(xi_scr)

    zeros = jnp.zeros((tt, hw), F32)
    for b in range(nb):
        ub = u_ref[b].astype(F32)
        for hh in range(hs):
            rows = pl.ds(hh * nb + b, tt, stride=SUBLANES)
            for c in range(hs):
                lhs_scr[c, rows, :] = ub[:, c * hw:(c + 1) * hw] if c == hh else zeros

    lhs = jnp.concatenate([lhs_scr[c] for c in range(hs)], axis=-1).astype(BF16)
    hm = lhs.shape[0] // 2
    for r0 in (0, hm):
        sr_scr[r0:r0 + hm, :] = jnp.dot(lhs[r0:r0 + hm], rb_ref[0, :, :sw],
                                        preferred_element_type=F32)
        si_scr[r0:r0 + hm, :] = jnp.dot(lhs[r0:r0 + hm], rb_ref[0, :, sw:],
                                        preferred_element_type=F32)

    ar, ai = ar_ref[0], ai_ref[0]

    def step(t, carry):
        xr, xi = carry
        rows = pl.ds(pl.multiple_of(t * SUBLANES, SUBLANES), SUBLANES)
        nxr = ar * xr - ai * xi + sr_scr[rows, :]
        nxi = ar * xi + ai * xr + si_scr[rows, :]
        sr_scr[rows, :] = nxr
        si_scr[rows, :] = nxi
        return nxr, nxi

    xr, xi = lax.fori_loop(0, tt, step, (xr_scr[...], xi_scr[...]), unroll=8)
    xr_scr[...] = xr
    xi_scr[...] = xi

    for r0 in (0, hm):
        o2 = (jnp.dot(sr_scr[r0:r0 + hm, :].astype(BF16), rc_ref[0, :sw, :],
                      preferred_element_type=F32)
              + jnp.dot(si_scr[r0:r0 + hm, :].astype(BF16), rc_ref[0, sw:, :],
                        preferred_element_type=F32))
        for c in range(hs):
            o2_scr[c, r0:r0 + hm, :] = o2[:, c * hw:(c + 1) * hw]

    for b in range(nb):
        parts = [o2_scr[hh, pl.ds(hh * nb + b, tt, stride=SUBLANES), :] for hh in range(hs)]
        yb = jnp.concatenate(parts, axis=-1) + d_ref[...] * u_ref[b].astype(F32)
        y_ref[b] = _gelu_tanh(yb).astype(y_ref.dtype)


def s5_core(z3, abar_r, abar_i, bbar_r, bbar_i, c_re, c_im, d_skip, *, tt=256, side_casts=None,
            side_norm=None):
    nb, seq, _ = z3.shape
    g, n, p = bbar_r.shape
    assert SUBLANES % nb == 0
    hs = SUBLANES // nb
    gph = GROUPS_PER_HALF
    assert g % (hs * gph) == 0
    nblk = g // (hs * gph)
    hw, sw = gph * p, gph * n
    cw = hs * hw
    assert hw == LANES and sw % LANES == 0
    tt = min(tt, seq)
    assert seq % tt == 0

    eye = jnp.eye(gph, dtype=F32)

    def in_proj(bb):
        bb = bb.reshape(nblk, hs, gph, n, p)
        return jnp.einsum("jhqnp,qr->jhqprn", bb, eye).reshape(nblk, cw, sw)

    def out_proj(cc):
        cc = cc.reshape(nblk, hs, gph, p, n)
        return jnp.einsum("jhrpn,qr->jqnhrp", cc, eye).reshape(nblk, sw, cw)

    rb = jnp.concatenate([in_proj(bbar_r), in_proj(bbar_i)], axis=-1).astype(BF16)
    rc = jnp.concatenate([out_proj(c_re), out_proj(-c_im)], axis=1).astype(BF16)

    def rows(a):
        a = a.reshape(nblk, hs, 1, sw)
        return jnp.broadcast_to(a, (nblk, hs, nb, sw)).reshape(nblk, hs * nb, sw)

    (y,), side = _pallas_with_side_casts(
        functools.partial(_s5_kernel, nb=nb, hs=hs, tt=tt, sw=sw, hw=hw),
        grid=(nblk, seq // tt),
        in_specs=[pl.BlockSpec((nb, tt, cw), lambda j, t: (0, t, j)),
                  pl.BlockSpec((1, cw, 2 * sw), lambda j, t: (j, 0, 0)),
                  pl.BlockSpec((1, 2 * sw, cw), lambda j, t: (j, 0, 0)),
                  pl.BlockSpec((1, SUBLANES, sw), lambda j, t: (j, 0, 0)),
                  pl.BlockSpec((1, SUBLANES, sw), lambda j, t: (j, 0, 0)),
                  pl.BlockSpec((1, cw), lambda j, t: (0, j))],
        out_specs=[pl.BlockSpec((nb, tt, cw), lambda j, t: (0, t, j))],
        out_shape=[jax.ShapeDtypeStruct((nb, seq, g * p), BF16)],
        args=[z3, rb, rc, rows(abar_r), rows(abar_i), d_skip.reshape(1, g * p).astype(F32)],
        side_casts=side_casts or (),
        side_norm=side_norm,
        scratch_shapes=[pltpu.VMEM((hs, SUBLANES * tt, hw), F32),
                        pltpu.VMEM((SUBLANES * tt, sw), F32),
                        pltpu.VMEM((SUBLANES * tt, sw), F32),
                        pltpu.VMEM((hs, SUBLANES * tt, hw), F32),
                        pltpu.VMEM((SUBLANES, sw), F32),
                        pltpu.VMEM((SUBLANES, sw), F32)],
        compiler_params=_params("parallel", "arbitrary"),
        name="s5_core",
    )
    return y if side_casts is None and side_norm is None else (y, *side)


def _glu_kernel(y_ref, w_ref, g_ref, into_ref, o_ref):
    del into_ref
    y = y_ref[...]
    v = jnp.dot(y, w_ref[...], preferred_element_type=F32)
    o = y.astype(F32) * jax.nn.sigmoid(v)
    o_ref[...] = _rms(o, g_ref[...]).astype(o_ref.dtype)


def glu_norm(y, w, g, *, into, tm=512):
    m, k = y.shape
    tm = min(tm, m)
    assert into.shape[0] == m and into.shape[1] % k == 0 and into.dtype == BF16
    return pl.pallas_call(
        _glu_kernel,
        grid=(m // tm,),
        in_specs=[pl.BlockSpec((tm, k), lambda i: (i, 0)),
                  pl.BlockSpec((k, k), lambda i: (0, 0)),
                  pl.BlockSpec((1, k), lambda i: (0, 0)),
                  pl.BlockSpec(memory_space=pl.ANY)],
        out_specs=pl.BlockSpec((tm, k), lambda i: (i, 0)),
        out_shape=jax.ShapeDtypeStruct(into.shape, BF16),
        input_output_aliases={3: 0},
        compiler_params=_params("parallel"),
        name="glu_norm",
    )(y, w, g.reshape(1, k).astype(F32), into)


LOG2E = math.log2(math.e)
def _cumsum_kernel(f_ref, bf_ref, c_ref, *, chunk):
    seq = f_ref.shape[1]
    row = lax.broadcasted_iota(jnp.int32, (chunk, chunk), 0)
    col = lax.broadcasted_iota(jnp.int32, (chunk, chunk), 1)
    tri = (col <= row).astype(F32)
    carry = jnp.zeros((1, f_ref.shape[2]), F32)
    for ci in range(seq // chunk):
        sl = slice(ci * chunk, (ci + 1) * chunk)
        x = f_ref[0, sl, :] + bf_ref[...]
        log_f = (jnp.minimum(x, 0.0) - jnp.log1p(jnp.exp(-jnp.abs(x)))) * LOG2E
        cs = jnp.dot(tri, log_f, precision=lax.Precision.HIGHEST,
                     preferred_element_type=F32) + carry
        c_ref[0, sl, :] = cs
        carry = cs[chunk - 1:chunk, :]


def forget_cumsum(f3, b_f_row, *, chunk=256):
    nb, seq, w = f3.shape
    chunk = min(chunk, seq)
    return pl.pallas_call(
        functools.partial(_cumsum_kernel, chunk=chunk),
        grid=(nb,),
        in_specs=[pl.BlockSpec((1, seq, w), lambda b: (b, 0, 0)),
                  pl.BlockSpec((1, w), lambda b: (0, 0))],
        out_specs=pl.BlockSpec((1, seq, w), lambda b: (b, 0, 0)),
        out_shape=jax.ShapeDtypeStruct((nb, seq, w), F32),
        compiler_params=_params("parallel"),
        name="forget_cumsum",
    )(f3, b_f_row)


FOX_HEADS_PER_STEP = 2


def _fox_kernel(q_ref, k_ref, v_ref, cq_ref, ck_ref, o_ref, *, tq, big, nh, dh):
    i = pl.program_id(2)
    per_big = big // tq

    def scores(hd, start, width, diagonal):
        cols = slice(hd * dh, (hd + 1) * dh)
        rows = pl.ds(start, width)
        s = lax.dot_general(q_ref[0, :, cols], k_ref[0, rows, cols], (((1,), (1,)), ((), ())),
                            preferred_element_type=F32)
        s = s + (cq_ref[0, hd] - ck_ref[0, hd, :, rows])
        if diagonal:
            r = lax.broadcasted_iota(jnp.int32, s.shape, 0)
            c = lax.broadcasted_iota(jnp.int32, s.shape, 1)
            s = jnp.where(c <= r, s, NEG_BIG)
        return s

    def update(hd, s, start, width, carry):
        m, l, acc = carry
        v = v_ref[0, pl.ds(start, width), hd * dh:(hd + 1) * dh]
        m_new = jnp.maximum(m, jnp.max(s, axis=-1, keepdims=True))
        alpha = jnp.exp2(m - m_new)
        p = jnp.exp2(s - m_new)
        l = alpha * l + jnp.sum(p, axis=-1, keepdims=True)
        acc = alpha * acc + jnp.dot(p.astype(v.dtype), v, preferred_element_type=F32)
        return m_new, l, acc

    def block(start, width, carries, diagonal):
        ss = [scores(hd, start, width, diagonal) for hd in range(nh)]
        return tuple(update(hd, ss[hd], start, width, carries[hd]) for hd in range(nh))

    init = tuple((jnp.full((tq, 1), NEG_BIG, F32), jnp.zeros((tq, 1), F32),
                  jnp.zeros((tq, dh), F32)) for _ in range(nh))
    n_big = i // per_big
    carries = lax.fori_loop(
        0, n_big, lambda j, c: block(pl.multiple_of(j * big, big), big, c, False), init)
    carries = lax.fori_loop(
        n_big * per_big, i, lambda j, c: block(pl.multiple_of(j * tq, tq), tq, c, False), carries)
    carries = block(pl.multiple_of(i * tq, tq), tq, carries, True)
    for hd in range(nh):
        _, l, acc = carries[hd]
        o_ref[0, :, hd * dh:(hd + 1) * dh] = (acc / l).astype(o_ref.dtype)


def fox_attention(zq, zk, zv, c, *, q_off, k_off, v_off, heads, dh, tq=1024, big=1024,
                  side_casts=None):
    nb, seq, _ = zq.shape
    tq = min(tq, seq)
    big = min(big, seq)
    nq = seq // tq
    nh = FOX_HEADS_PER_STEP if heads % FOX_HEADS_PER_STEP == 0 else 1
    w = nh * dh
    assert seq % tq == 0 and big % tq == 0
    assert q_off % w == 0 and k_off % w == 0 and v_off % w == 0
    ch = jnp.transpose(c[:, :, :heads], (0, 2, 1))
    cq = ch.reshape(nb, heads, seq, 1)
    ck = ch.reshape(nb, heads, 1, seq)
    qb, kb, vb = q_off // w, k_off // w, v_off // w
    (out,), side = _pallas_with_side_casts(
        functools.partial(_fox_kernel, tq=tq, big=big, nh=nh, dh=dh),
        grid=(nb, heads // nh, nq),
        in_specs=[pl.BlockSpec((1, tq, w), lambda b, h, i: (b, i, qb + h)),
                  pl.BlockSpec((1, seq, w), lambda b, h, i: (b, 0, kb + h)),
                  pl.BlockSpec((1, seq, w), lambda b, h, i: (b, 0, vb + h)),
                  pl.BlockSpec((1, nh, tq, 1), lambda b, h, i: (b, h, i, 0)),
                  pl.BlockSpec((1, nh, 1, seq), lambda b, h, i: (b, h, 0, 0))],
        out_specs=[pl.BlockSpec((1, tq, w), lambda b, h, i: (b, i, h))],
        out_shape=[jax.ShapeDtypeStruct((nb, seq, heads * dh), F32)],
        args=[zq, zk, zv, cq, ck],
        side_casts=side_casts or (),
        compiler_params=_params("parallel", "parallel", "parallel"),
        name="fox_attention",
    )
    return out if side_casts is None else (out, *side)


def _xattn_kernel(q_ref, k_ref, v_ref, o_ref, *, heads, dh):
    scores = [lax.dot_general(q_ref[:, hd * dh:(hd + 1) * dh], k_ref[:, hd * dh:(hd + 1) * dh],
                              (((1,), (1,)), ((), ())), preferred_element_type=F32)
              for hd in range(heads)]
    for hd in range(heads):
        sl = slice(hd * dh, (hd + 1) * dh)
        s = scores[hd]
        s = s - jnp.max(s, axis=-1, keepdims=True)
        p = jnp.exp(s)
        p = p / jnp.sum(p, axis=-1, keepdims=True)
        o_ref[:, sl] = jnp.dot(p.astype(BF16), v_ref[:, sl],
                               preferred_element_type=F32).astype(o_ref.dtype)


def cross_attention(q, k, v, *, nb, heads, tm=512):
    t, w = q.shape
    seq, mlen = t // nb, k.shape[0] // nb
    tm = min(tm, seq)
    nt = seq // tm
    return pl.pallas_call(
        functools.partial(_xattn_kernel, heads=heads, dh=w // heads),
        grid=(nb, nt),
        in_specs=[pl.BlockSpec((tm, w), lambda b, i: (b * nt + i, 0)),
                  pl.BlockSpec((mlen, w), lambda b, i: (b, 0)),
                  pl.BlockSpec((mlen, w), lambda b, i: (b, 0))],
        out_specs=pl.BlockSpec((tm, w), lambda b, i: (b * nt + i, 0)),
        out_shape=jax.ShapeDtypeStruct((t, w), BF16),
        compiler_params=_params("parallel", "parallel"),
        name="cross_attention",
    )(q, k, v)


def kernel(x, mem, ffn1_norm, ffn1_w_gate, ffn1_w_up, ffn1_w_down, mix_norm, w_in, ssm_A_re, ssm_A_im, ssm_log_step, ssm_B_re, ssm_B_im, ssm_C_re, ssm_C_im, ssm_D, ssm_w_glu, ssm_out_norm, fox_b_f, fox_q_norm, fox_k_norm, fox_out_norm, w_out, xattn_norm, mem_norm, xattn_wq, xattn_wk, xattn_wv, xattn_q_norm, xattn_k_norm, xattn_wo, ffn2_norm, ffn2_w_gate, ffn2_w_up, ffn2_w_down, final_norm):
    nb, seq, d = x.shape
    t = nb * seq
    depth = ffn1_norm.shape[0]
    g, n_state = ssm_A_re.shape[1:]
    p = ssm_B_re.shape[-1]
    w_ssm = g * p
    fox_h, fox_dh = fox_b_f.shape[-1], fox_q_norm.shape[-1]
    w_fox = fox_h * fox_dh
    o_q, o_k, o_v, o_f = w_ssm, w_ssm + w_fox, w_ssm + 2 * w_fox, w_ssm + 3 * w_fox
    x_dh = xattn_q_norm.shape[-1]
    x_w = xattn_wq.shape[-1]
    x_heads = x_w // x_dh
    mlen = mem.shape[1]
    tn_in = min(1024, w_ssm, w_fox)
    assert w_ssm % w_fox == 0 and w_ssm % tn_in == 0 and w_fox % tn_in == 0
    uv_lo_tiles, qk_tiles = w_ssm // tn_in, 2 * w_fox // tn_in

    h = x.reshape(t, d)
    mem2 = mem.reshape(nb * mlen, d)

    ffn1_w = (cast_bf16(ffn1_w_gate, 0), cast_bf16(ffn1_w_up, 0),
              cast_bf16(ffn1_w_down, 0, scale=0.5))
    w_in_b = cast_bf16(w_in, 0)

    for l in range(depth):
        more = l + 1 < depth
        h = ffn(h, ffn1_norm[l], *ffn1_w)

        hn = rmsnorm(h, mix_norm[l], BF16)
        gain_qk = jnp.concatenate([
            jnp.tile(fox_q_norm[l].astype(F32) * (fox_dh ** -0.5 * LOG2E), fox_h),
            jnp.tile(fox_k_norm[l].astype(F32), fox_h)])
        w_f = jnp.pad(w_in_b[:, o_f:], ((0, 0), (0, LANES - fox_h)))
        z_qk, w2_gate = matmul(
            hn, w_in_b, BF16, tm=1024, tn=tn_in, n_cols=2 * w_fox,
            w_col_tile=lambda j: j + uv_lo_tiles, norm_gain=gain_qk, norm_group=fox_dh,
            side_casts=[(ffn2_w_gate, l, 1.0)])
        z_uv, f_logit, w2_up = matmul(
            hn, w_in_b, BF16, tm=1024, tn=tn_in, n_cols=w_ssm + w_fox,
            w_col_tile=lambda j: jnp.where(j < uv_lo_tiles, j, j + qk_tiles), extra_w=w_f,
            side_casts=[(ffn2_w_up, l, 1.0)])
        z_qk3 = z_qk.reshape(nb, seq, 2 * w_fox)
        z3 = z_uv.reshape(nb, seq, w_ssm + w_fox)

        b_f_row = jnp.pad(fox_b_f[l].astype(F32), (0, LANES - fox_h)).reshape(1, LANES)
        c = forget_cumsum(f_logit.reshape(nb, seq, LANES), b_f_row)
        y_fox, wv_b, wo_b, w2_down, *next_rest = fox_attention(
            z_qk3, z_qk3, z3, c, q_off=0, k_off=w_fox, v_off=w_ssm, heads=fox_h, dh=fox_dh,
            side_casts=[(xattn_wv, l, 1.0), (xattn_wo, l, 1.0), (ffn2_w_down, l, 0.5)]
            + ([(ffn1_w_down, l + 1, 0.5), (w_in, l + 1, 1.0)] if more else []))
        ffn2_w = (w2_gate, w2_up, w2_down)

        abar_r, abar_i, bbar_r, bbar_i = s5_discretise(
            ssm_A_re[l], ssm_A_im[l], ssm_log_step[l], ssm_B_re[l], ssm_B_im[l])
        y_ssm, w_glu_b, w_out_b, wq_b, wk_b, *next_gate_up, y = s5_core(
            z3, abar_r, abar_i, bbar_r, bbar_i,
            ssm_C_re[l].astype(F32), ssm_C_im[l].astype(F32), ssm_D[l],
            side_casts=[(ssm_w_glu, l, 1.0), (w_out, l, 1.0), (xattn_wq, l, 1.0),
                        (xattn_wk, l, 1.0)]
            + ([(ffn1_w_gate, l + 1, 1.0), (ffn1_w_up, l + 1, 1.0)] if more else []),
            side_norm=(y_fox.reshape(t, w_fox), fox_out_norm[l], w_ssm + w_fox, w_ssm // w_fox))
        y = glu_norm(y_ssm.reshape(t, w_ssm), w_glu_b, ssm_out_norm[l], into=y)
        if more:
            ffn1_w = (*next_gate_up, next_rest[0])
            w_in_b = next_rest[1]

        h = matmul(y, w_out_b, F32, tm=1024, tn=1024, res=h)

        hn = rmsnorm(h, xattn_norm[l], BF16)
        mn = rmsnorm(mem2, mem_norm[l], BF16)
        q = matmul(hn, wq_b, BF16, tm=1024, tn=x_dh,
                   norm_gain=jnp.tile(xattn_q_norm[l].astype(F32) * (x_dh ** -0.5), x_heads),
                   norm_group=x_dh)
        k = matmul(mn, wk_b, BF16, tm=1024, tn=x_dh,
                   norm_gain=jnp.tile(xattn_k_norm[l].astype(F32), x_heads), norm_group=x_dh)
        v = matmul(mn, wv_b, BF16, tm=1024, tn=x_dh)
        o = cross_attention(q, k, v, nb=nb, heads=x_heads)
        h = matmul(o, wo_b, F32, tm=1024, tn=1024, res=h)

        h = ffn(h, ffn2_norm[l], *ffn2_w, final_gain=final_norm[l])

    return h.reshape(nb, seq, d)
```

```python
import functools
import math

import jax
import jax.numpy as jnp
from jax import lax
from jax.experimental import pallas as pl
from jax.experimental.pallas import tpu as pltpu

F32 = jnp.float32
BF16 = jnp.bfloat16
EPS = 1e-6
NEG_BIG = -1e30

V7X_VMEM_LIMIT_BYTES = 58 * 1024 * 1024
LANES = 128
SUBLANES = 8


def _params(*sem):
    return pltpu.CompilerParams(dimension_semantics=sem,
                                vmem_limit_bytes=V7X_VMEM_LIMIT_BYTES)


def _rms(x, g):
    ms = jnp.mean(x * x, axis=-1, keepdims=True)
    return x * lax.rsqrt(ms + EPS) * g


def _cast_kernel(w_ref, o_ref, *, scale):
    w = w_ref[...]
    if scale != 1.0:
        w = w * scale
    o_ref[...] = w.astype(o_ref.dtype)


def cast_bf16(w_stack, layer, *, scale=1.0, tr=256):
    _, k, n = w_stack.shape
    tr = min(tr, k)
    assert k % tr == 0
    return pl.pallas_call(
        functools.partial(_cast_kernel, scale=scale),
        grid=(k // tr,),
        in_specs=[pl.BlockSpec((None, tr, n), lambda i: (layer, i, 0))],
        out_specs=pl.BlockSpec((tr, n), lambda i: (i, 0)),
        out_shape=jax.ShapeDtypeStruct((k, n), BF16),
        compiler_params=_params("parallel"),
        name="cast_bf16",
    )(w_stack)

BF16_SUBLANE_TILE = 16


def _side_block(k, n, steps):
    if k % steps == 0 and (k // steps) % BF16_SUBLANE_TILE == 0:
        return (k // steps, n), lambda s: (s, 0)
    for s1 in (16, 8, 4, 2, 1):
        s2 = steps // s1
        if (steps % s1 == 0 and k % s1 == 0 and (k // s1) % BF16_SUBLANE_TILE == 0
                and n % s2 == 0 and (n // s2) % LANES == 0):
            return (k // s1, n // s2), lambda s: (s // s2, s % s2)
    raise ValueError(f"cannot tile a ({k}, {n}) weight in {steps} blocks")


def _pallas_with_side_casts(body, *, grid, in_specs, out_specs, out_shape, args, side_casts=(),
                            side_norm=None, **kwargs):
    out_specs, out_shape = list(out_specs), list(out_shape)
    n_in, n_out, n_side = len(in_specs), len(out_specs), len(side_casts)
    steps = math.prod(grid)
    strides = [math.prod(grid[a + 1:]) for a in range(len(grid))]

    def linear_step(*g):
        return sum(gi * st for gi, st in zip(g, strides))

    if side_norm is not None:
        nx, ngain, out_cols, col_block = side_norm
        nm, nk = nx.shape
        assert nm % steps == 0 and (nm // steps) % BF16_SUBLANE_TILE == 0
        norm_in_specs = [pl.BlockSpec((nm // steps, nk), lambda *g: (linear_step(*g), 0)),
                         pl.BlockSpec((1, nk), lambda *g: (0, 0))]
        norm_out_spec = pl.BlockSpec((nm // steps, nk), lambda *g: (linear_step(*g), col_block))
        norm_out_shape = jax.ShapeDtypeStruct((nm, out_cols), BF16)
        norm_args = [nx, ngain.reshape(1, nk).astype(F32)]
    scales = []
    for w_stack, layer, scale in side_casts:
        _, k, n = w_stack.shape
        blk, to_index = _side_block(k, n, steps)

        def index(*g, _to_index=to_index):
            return _to_index(linear_step(*g))

        in_specs = in_specs + [pl.BlockSpec((None,) + blk,
                                            lambda *g, _l=layer, _ix=index: (_l,) + _ix(*g))]
        out_specs.append(pl.BlockSpec(blk, index))
        out_shape.append(jax.ShapeDtypeStruct((k, n), BF16))
        args = list(args) + [w_stack]
        scales.append(scale)

    n_norm_in = 0
    if side_norm is not None:
        n_norm_in = len(norm_in_specs)
        in_specs = in_specs + norm_in_specs
        out_specs.append(norm_out_spec)
        out_shape.append(norm_out_shape)
        args = list(args) + norm_args
    n_all_in = n_in + n_side + n_norm_in
    n_side_out = n_side + (1 if side_norm is not None else 0)

    def kern(*refs):
        side_in = refs[n_in:n_in + n_side]
        outs = refs[n_all_in:n_all_in + n_out]
        side_out = refs[n_all_in + n_out:n_all_in + n_out + n_side_out]
        scratch = refs[n_all_in + n_out + n_side_out:]
        for si, so, scale in zip(side_in, side_out, scales):
            w = si[...]
            so[...] = (w if scale == 1.0 else w * scale).astype(so.dtype)
        if side_norm is not None:
            nx_ref, ng_ref = refs[n_in + n_side:n_all_in]
            side_out[-1][...] = _rms(nx_ref[...], ng_ref[...]).astype(BF16)
        body(*refs[:n_in], *outs, *scratch)

    res = pl.pallas_call(kern, grid=grid, in_specs=in_specs, out_specs=out_specs,
                         out_shape=out_shape, **kwargs)(*args)
    return tuple(res[:n_out]), tuple(res[n_out:])


def _rmsnorm_kernel(x_ref, g_ref, o_ref):
    o_ref[...] = _rms(x_ref[...].astype(F32), g_ref[...]).astype(o_ref.dtype)


def rmsnorm(x, g, out_dtype, tm=256):
    m, k = x.shape
    tm = min(tm, m)
    return pl.pallas_call(
        _rmsnorm_kernel,
        grid=(m // tm,),
        in_specs=[pl.BlockSpec((tm, k), lambda i: (i, 0)),
                  pl.BlockSpec((1, k), lambda i: (0, 0))],
        out_specs=pl.BlockSpec((tm, k), lambda i: (i, 0)),
        out_shape=jax.ShapeDtypeStruct((m, k), out_dtype),
        compiler_params=_params("parallel"),
        name="rmsnorm",
    )(x, g.reshape(1, k).astype(F32))


NORM_ROW_CHUNKS = 4


def _mm_kernel(*refs, has_res, norm_group, has_extra):
    x_ref, w_ref = refs[0], refs[1]
    pos = 2
    res_ref = g_ref = None
    if has_res:
        res_ref = refs[pos]
        pos += 1
    if norm_group:
        g_ref = refs[pos]
        pos += 1
    if has_extra:
        xw_ref, o_ref, xo_ref = refs[pos], refs[pos + 1], refs[pos + 2]

        @pl.when(pl.program_id(1) == 0)
        def _():
            xo_ref[...] = jnp.dot(x_ref[...], xw_ref[...], preferred_element_type=F32)
    else:
        o_ref = refs[pos]

    if not norm_group:
        acc = jnp.dot(x_ref[...], w_ref[...], preferred_element_type=F32)
        if has_res:
            acc = acc + res_ref[...]
        o_ref[...] = acc.astype(o_ref.dtype)
        return

    tm = x_ref.shape[0]
    chunks = NORM_ROW_CHUNKS if tm % (NORM_ROW_CHUNKS * BF16_SUBLANE_TILE) == 0 else 1
    ch = tm // chunks
    accs = [jnp.dot(x_ref[r * ch:(r + 1) * ch, :], w_ref[...], preferred_element_type=F32)
            for r in range(chunks)]
    for r, acc in enumerate(accs):
        rows = slice(r * ch, (r + 1) * ch)
        if has_res:
            acc = acc + res_ref[rows, :]
        for c in range(acc.shape[1] // norm_group):
            sl = slice(c * norm_group, (c + 1) * norm_group)
            o_ref[rows, sl] = _rms(acc[:, sl], g_ref[:, sl]).astype(o_ref.dtype)


def matmul(x, w, out_dtype, *, tm=512, tn=512, n_cols=None, w_col_tile=None, res=None,
           norm_gain=None, norm_group=0, side_casts=None, extra_w=None):
    m, k = x.shape
    n = w.shape[1] if n_cols is None else n_cols
    tm, tn = min(tm, m), min(tn, n)
    assert m % tm == 0 and n % tn == 0
    if w_col_tile is None:
        w_col_tile = lambda j: j
    in_specs = [pl.BlockSpec((tm, k), lambda i, j: (i, 0)),
                pl.BlockSpec((k, tn), lambda i, j: (0, w_col_tile(j)))]
    args = [x, w]
    if res is not None:
        in_specs.append(pl.BlockSpec((tm, tn), lambda i, j: (i, j)))
        args.append(res)
    if norm_group:
        assert tn % norm_group == 0
        in_specs.append(pl.BlockSpec((1, tn), lambda i, j: (0, j)))
        args.append(norm_gain.reshape(1, n).astype(F32))
    out_specs = [pl.BlockSpec((tm, tn), lambda i, j: (i, j))]
    out_shape = [jax.ShapeDtypeStruct((m, n), out_dtype)]
    if extra_w is not None:
        assert side_casts is not None
        e = extra_w.shape[1]
        in_specs.append(pl.BlockSpec((k, e), lambda i, j: (0, 0)))
        args.append(extra_w)
        out_specs.append(pl.BlockSpec((tm, e), lambda i, j: (i, 0)))
        out_shape.append(jax.ShapeDtypeStruct((m, e), F32))
    outs, side = _pallas_with_side_casts(
        functools.partial(_mm_kernel, has_res=res is not None, norm_group=norm_group,
                          has_extra=extra_w is not None),
        grid=(m // tm, n // tn),
        in_specs=in_specs,
        out_specs=out_specs,
        out_shape=out_shape,
        args=args,
        side_casts=side_casts or (),
        compiler_params=_params("parallel", "arbitrary" if extra_w is not None else "parallel"),
        name="matmul",
    )
    return outs[0] if side_casts is None else (*outs, *side)


def _ffn_kernel(*refs, final_norm):
    if final_norm:
        x_ref, g_ref, wg_ref, wu_ref, wd_ref, fg_ref, o_ref, n_scr = refs
    else:
        x_ref, g_ref, wg_ref, wu_ref, wd_ref, o_ref, n_scr = refs
        fg_ref = None
    j = pl.program_id(1)

    @pl.when(j == 0)
    def _():
        x = x_ref[...]
        n_scr[...] = _rms(x, g_ref[...]).astype(BF16)
        o_ref[...] = x

    n = n_scr[...]
    gate = jnp.dot(n, wg_ref[...], preferred_element_type=F32)
    up = jnp.dot(n, wu_ref[...], preferred_element_type=F32)
    hid = (gate * jax.nn.sigmoid(gate) * up).astype(BF16)
    o_ref[...] += jnp.dot(hid, wd_ref[...], preferred_element_type=F32)

    if final_norm:
        @pl.when(j == pl.num_programs(1) - 1)
        def _():
            o_ref[...] = _rms(o_ref[...], fg_ref[...])


def ffn(x, g, wg, wu, wd_half, final_gain=None, *, tm=512, tf=256):
    m, d = x.shape
    dff = wg.shape[1]
    tm, tf = min(tm, m), min(tf, dff)
    assert m % tm == 0 and dff % tf == 0
    in_specs = [pl.BlockSpec((tm, d), lambda i, j: (i, 0)),
                pl.BlockSpec((1, d), lambda i, j: (0, 0)),
                pl.BlockSpec((d, tf), lambda i, j: (0, j)),
                pl.BlockSpec((d, tf), lambda i, j: (0, j)),
                pl.BlockSpec((tf, d), lambda i, j: (j, 0))]
    args = [x, g.reshape(1, d).astype(F32), wg, wu, wd_half]
    if final_gain is not None:
        in_specs.append(pl.BlockSpec((1, d), lambda i, j: (0, 0)))
        args.append(final_gain.reshape(1, d).astype(F32))
    return pl.pallas_call(
        functools.partial(_ffn_kernel, final_norm=final_gain is not None),
        grid=(m // tm, dff // tf),
        in_specs=in_specs,
        out_specs=pl.BlockSpec((tm, d), lambda i, j: (i, 0)),
        out_shape=jax.ShapeDtypeStruct((m, d), F32),
        scratch_shapes=[pltpu.VMEM((tm, d), BF16)],
        compiler_params=_params("parallel", "arbitrary"),
        name="ffn",
    )(*args)


def _s5_disc_kernel(are_ref, aim_ref, ls_ref, bre_ref, bim_ref,
                    abr_ref, abi_ref, bbr_ref, bbi_ref):
    ar, ai = are_ref[...], aim_ref[...]
    dt = jnp.exp(ls_ref[...])
    mag = jnp.exp(ar * dt)
    abar_r = mag * jnp.cos(ai * dt)
    abar_i = mag * jnp.sin(ai * dt)
    den = ar * ar + ai * ai
    pr, pi_ = abar_r - 1.0, abar_i
    coef_r = (pr * ar + pi_ * ai) / den
    coef_i = (pi_ * ar - pr * ai) / den
    br, bi = bre_ref[...], bim_ref[...]
    abr_ref[...] = abar_r
    abi_ref[...] = abar_i
    bbr_ref[...] = coef_r * br - coef_i * bi
    bbi_ref[...] = coef_r * bi + coef_i * br


def s5_discretise(a_re, a_im, log_step, b_re, b_im):
    g, n = a_re.shape
    p = b_re.shape[-1]
    col = jax.ShapeDtypeStruct((g, n, 1), F32)
    mat = jax.ShapeDtypeStruct((g, n, p), F32)
    ls = jnp.broadcast_to(log_step.reshape(g, 1, 1), (g, n, 1))
    abr, abi, bbr, bbi = pl.pallas_call(
        _s5_disc_kernel,
        out_shape=(col, col, mat, mat),
        compiler_params=pltpu.CompilerParams(vmem_limit_bytes=V7X_VMEM_LIMIT_BYTES),
        name="s5_discretise",
    )(a_re.reshape(g, n, 1), a_im.reshape(g, n, 1), ls, b_re, b_im)
    return abr.reshape(g, n), abi.reshape(g, n), bbr, bbi


GROUPS_PER_HALF = 8


def _gelu_tanh(x):
    c = math.sqrt(2.0 / math.pi)
    return 0.5 * x * (1.0 + jnp.tanh(c * (x + 0.044715 * (x * x * x))))


def _s5_kernel(u_ref, rb_ref, rc_ref, ar_ref, ai_ref, d_ref, y_ref,
               lhs_scr, sr_scr, si_scr, o2_scr, xr_scr, xi_scr, *, nb, hs, tt, sw, hw):
    @pl.when(pl.program_id(1) == 0)
    def _():
        xr_scr[...] = jnp.zeros_like(xr_scr)
        xi_scr[...] = jnp.zeros_like(xi_scr)

    zeros = jnp.zeros((tt, hw), F32)
    for b in range(nb):
        ub = u_ref[b].astype(F32)
        for hh in range(hs):
            rows = pl.ds(hh * nb + b, tt, stride=SUBLANES)
            for c in range(hs):
                lhs_scr[c, rows, :] = ub[:, c * hw:(c + 1) * hw] if c == hh else zeros

    lhs = jnp.concatenate([lhs_scr[c] for c in range(hs)], axis=-1).astype(BF16)
    hm = lhs.shape[0] // 2
    for r0 in (0, hm):
        sr_scr[r0:r0 + hm, :] = jnp.dot(lhs[r0:r0 + hm], rb_ref[0, :, :sw],
                                        preferred_element_type=F32)
        si_scr[r0:r0 + hm, :] = jnp.dot(lhs[r0:r0 + hm], rb_ref[0, :, sw:],
                                        preferred_element_type=F32)

    ar, ai = ar_ref[0], ai_ref[0]

    def step(t, carry):
        xr, xi = carry
        rows = pl.ds(pl.multiple_of(t * SUBLANES, SUBLANES), SUBLANES)
        nxr = ar * xr - ai * xi + sr_scr[rows, :]
        nxi = ar * xi + ai * xr + si_scr[rows, :]
        sr_scr[rows, :] = nxr
        si_scr[rows, :] = nxi
        return nxr, nxi

    xr, xi = lax.fori_loop(0, tt, step, (xr_scr[...], xi_scr[...]), unroll=8)
    xr_scr[...] = xr
    xi_scr[...] = xi

    for r0 in (0, hm):
        o2 = (jnp.dot(sr_scr[r0:r0 + hm, :].astype(BF16), rc_ref[0, :sw, :],
                      preferred_element_type=F32)
              + jnp.dot(si_scr[r0:r0 + hm, :].astype(BF16), rc_ref[0, sw:, :],
                        preferred_element_type=F32))
        for c in range(hs):
            o2_scr[c, r0:r0 + hm, :] = o2[:, c * hw:(c + 1) * hw]

    for b in range(nb):
        parts = [o2_scr[hh, pl.ds(hh * nb + b, tt, stride=SUBLANES), :] for hh in range(hs)]
        yb = jnp.concatenate(parts, axis=-1) + d_ref[...] * u_ref[b].astype(F32)
        y_ref[b] = _gelu_tanh(yb).astype(y_ref.dtype)


def s5_core(z3, abar_r, abar_i, bbar_r, bbar_i, c_re, c_im, d_skip, *, tt=256, side_casts=None,
            side_norm=None):
    nb, seq, _ = z3.shape
    g, n, p = bbar_r.shape
    assert SUBLANES % nb == 0
    hs = SUBLANES // nb
    gph = GROUPS_PER_HALF
    assert g % (hs * gph) == 0
    nblk = g // (hs * gph)
    hw, sw = gph * p, gph * n
    cw = hs * hw
    assert hw == LANES and sw % LANES == 0
    tt = min(tt, seq)
    assert seq % tt == 0

    eye = jnp.eye(gph, dtype=F32)

    def in_proj(bb):
        bb = bb.reshape(nblk, hs, gph, n, p)
        return jnp.einsum("jhqnp,qr->jhqprn", bb, eye).reshape(nblk, cw, sw)

    def out_proj(cc):
        cc = cc.reshape(nblk, hs, gph, p, n)
        return jnp.einsum("jhrpn,qr->jqnhrp", cc, eye).reshape(nblk, sw, cw)

    rb = jnp.concatenate([in_proj(bbar_r), in_proj(bbar_i)], axis=-1).astype(BF16)
    rc = jnp.concatenate([out_proj(c_re), out_proj(-c_im)], axis=1).astype(BF16)

    def rows(a):
        a = a.reshape(nblk, hs, 1, sw)
        return jnp.broadcast_to(a, (nblk, hs, nb, sw)).reshape(nblk, hs * nb, sw)

    (y,), side = _pallas_with_side_casts(
        functools.partial(_s5_kernel, nb=nb, hs=hs, tt=tt, sw=sw, hw=hw),
        grid=(nblk, seq // tt),
        in_specs=[pl.BlockSpec((nb, tt, cw), lambda j, t: (0, t, j)),
                  pl.BlockSpec((1, cw, 2 * sw), lambda j, t: (j, 0, 0)),
                  pl.BlockSpec((1, 2 * sw, cw), lambda j, t: (j, 0, 0)),
                  pl.BlockSpec((1, SUBLANES, sw), lambda j, t: (j, 0, 0)),
                  pl.BlockSpec((1, SUBLANES, sw), lambda j, t: (j, 0, 0)),
                  pl.BlockSpec((1, cw), lambda j, t: (0, j))],
        out_specs=[pl.BlockSpec((nb, tt, cw), lambda j, t: (0, t, j))],
        out_shape=[jax.ShapeDtypeStruct((nb, seq, g * p), BF16)],
        args=[z3, rb, rc, rows(abar_r), rows(abar_i), d_skip.reshape(1, g * p).astype(F32)],
        side_casts=side_casts or (),
        side_norm=side_norm,
        scratch_shapes=[pltpu.VMEM((hs, SUBLANES * tt, hw), F32),
                        pltpu.VMEM((SUBLANES * tt, sw), F32),
                        pltpu.VMEM((SUBLANES * tt, sw), F32),
                        pltpu.VMEM((hs, SUBLANES * tt, hw), F32),
                        pltpu.VMEM((SUBLANES, sw), F32),
                        pltpu.VMEM((SUBLANES, sw), F32)],
        compiler_params=_params("parallel", "arbitrary"),
        name="s5_core",
    )
    return y if side_casts is None and side_norm is None else (y, *side)


def _glu_kernel(y_ref, w_ref, g_ref, into_ref, o_ref):
    del into_ref
    tm = y_ref.shape[0]
    chunks = NORM_ROW_CHUNKS if tm % (NORM_ROW_CHUNKS * BF16_SUBLANE_TILE) == 0 else 1
    ch = tm // chunks
    vs = [jnp.dot(y_ref[r * ch:(r + 1) * ch, :], w_ref[...], preferred_element_type=F32)
          for r in range(chunks)]
    for r, v in enumerate(vs):
        rows = slice(r * ch, (r + 1) * ch)
        o = y_ref[rows, :].astype(F32) * jax.nn.sigmoid(v)
        o_ref[rows, :] = _rms(o, g_ref[...]).astype(o_ref.dtype)


def glu_norm(y, w, g, *, into, tm=512):
    m, k = y.shape
    tm = min(tm, m)
    assert into.shape[0] == m and into.shape[1] % k == 0 and into.dtype == BF16
    return pl.pallas_call(
        _glu_kernel,
        grid=(m // tm,),
        in_specs=[pl.BlockSpec((tm, k), lambda i: (i, 0)),
                  pl.BlockSpec((k, k), lambda i: (0, 0)),
                  pl.BlockSpec((1, k), lambda i: (0, 0)),
                  pl.BlockSpec(memory_space=pl.ANY)],
        out_specs=pl.BlockSpec((tm, k), lambda i: (i, 0)),
        out_shape=jax.ShapeDtypeStruct(into.shape, BF16),
        input_output_aliases={3: 0},
        compiler_params=_params("parallel"),
        name="glu_norm",
    )(y, w, g.reshape(1, k).astype(F32), into)


LOG2E = math.log2(math.e)
def _cumsum_kernel(f_ref, bf_ref, c_ref, *, chunk):
    seq = f_ref.shape[1]
    row = lax.broadcasted_iota(jnp.int32, (chunk, chunk), 0)
    col = lax.broadcasted_iota(jnp.int32, (chunk, chunk), 1)
    tri = (col <= row).astype(F32)
    carry = jnp.zeros((1, f_ref.shape[2]), F32)
    for ci in range(seq // chunk):
        sl = slice(ci * chunk, (ci + 1) * chunk)
        x = f_ref[0, sl, :] + bf_ref[...]
        log_f = (jnp.minimum(x, 0.0) - jnp.log1p(jnp.exp(-jnp.abs(x)))) * LOG2E
        cs = jnp.dot(tri, log_f, precision=lax.Precision.HIGHEST,
                     preferred_element_type=F32) + carry
        c_ref[0, sl, :] = cs
        carry = cs[chunk - 1:chunk, :]


def forget_cumsum(f3, b_f_row, *, chunk=256):
    nb, seq, w = f3.shape
    chunk = min(chunk, seq)
    return pl.pallas_call(
        functools.partial(_cumsum_kernel, chunk=chunk),
        grid=(nb,),
        in_specs=[pl.BlockSpec((1, seq, w), lambda b: (b, 0, 0)),
                  pl.BlockSpec((1, w), lambda b: (0, 0))],
        out_specs=pl.BlockSpec((1, seq, w), lambda b: (b, 0, 0)),
        out_shape=jax.ShapeDtypeStruct((nb, seq, w), F32),
        compiler_params=_params("parallel"),
        name="forget_cumsum",
    )(f3, b_f_row)


FOX_HEADS_PER_STEP = 2


def _fox_kernel(q_ref, k_ref, v_ref, ck_ref, o_ref, *, tq, big, nh, dh):
    i = pl.program_id(2)
    per_big = big // tq

    q_rows = pl.ds(pl.multiple_of(i * tq, tq), tq)
    cq_cols = [jnp.transpose(jnp.broadcast_to(ck_ref[0, hd, :, q_rows], (LANES, tq)))[:, :1]
               for hd in range(nh)]

    def scores(hd, start, width, diagonal):
        cols = slice(hd * dh, (hd + 1) * dh)
        rows = pl.ds(start, width)
        s = lax.dot_general(q_ref[0, :, cols], k_ref[0, rows, cols], (((1,), (1,)), ((), ())),
                            preferred_element_type=F32)
        s = s + (cq_cols[hd] - ck_ref[0, hd, :, rows])
        if diagonal:
            r = lax.broadcasted_iota(jnp.int32, s.shape, 0)
            c = lax.broadcasted_iota(jnp.int32, s.shape, 1)
            s = jnp.where(c <= r, s, NEG_BIG)
        return s

    def update(hd, s, start, width, carry):
        m, l, acc = carry
        v = v_ref[0, pl.ds(start, width), hd * dh:(hd + 1) * dh]
        m_new = jnp.maximum(m, jnp.max(s, axis=-1, keepdims=True))
        alpha = jnp.exp2(m - m_new)
        p = jnp.exp2(s - m_new)
        l = alpha * l + jnp.sum(p, axis=-1, keepdims=True)
        acc = alpha * acc + jnp.dot(p.astype(v.dtype), v, preferred_element_type=F32)
        return m_new, l, acc

    def block(start, width, carries, diagonal):
        ss = [scores(hd, start, width, diagonal) for hd in range(nh)]
        return tuple(update(hd, ss[hd], start, width, carries[hd]) for hd in range(nh))

    init = tuple((jnp.full((tq, 1), NEG_BIG, F32), jnp.zeros((tq, 1), F32),
                  jnp.zeros((tq, dh), F32)) for _ in range(nh))
    n_big = i // per_big
    carries = lax.fori_loop(
        0, n_big, lambda j, c: block(pl.multiple_of(j * big, big), big, c, False), init)
    carries = lax.fori_loop(
        n_big * per_big, i, lambda j, c: block(pl.multiple_of(j * tq, tq), tq, c, False), carries)
    carries = block(pl.multiple_of(i * tq, tq), tq, carries, True)
    for hd in range(nh):
        _, l, acc = carries[hd]
        o_ref[0, :, hd * dh:(hd + 1) * dh] = (acc / l).astype(o_ref.dtype)


def fox_attention(zq, zk, zv, c, *, q_off, k_off, v_off, heads, dh, tq=1024, big=1024,
                  side_casts=None):
    nb, seq, _ = zq.shape
    tq = min(tq, seq)
    big = min(big, seq)
    nq = seq // tq
    nh = FOX_HEADS_PER_STEP if heads % FOX_HEADS_PER_STEP == 0 else 1
    w = nh * dh
    assert seq % tq == 0 and big % tq == 0
    assert q_off % w == 0 and k_off % w == 0 and v_off % w == 0
    ch = jnp.transpose(c[:, :, :heads], (0, 2, 1))
    ck = ch.reshape(nb, heads, 1, seq)
    qb, kb, vb = q_off // w, k_off // w, v_off // w
    (out,), side = _pallas_with_side_casts(
        functools.partial(_fox_kernel, tq=tq, big=big, nh=nh, dh=dh),
        grid=(nb, heads // nh, nq),
        in_specs=[pl.BlockSpec((1, tq, w), lambda b, h, i: (b, i, qb + h)),
                  pl.BlockSpec((1, seq, w), lambda b, h, i: (b, 0, kb + h)),
                  pl.BlockSpec((1, seq, w), lambda b, h, i: (b, 0, vb + h)),
                  pl.BlockSpec((1, nh, 1, seq), lambda b, h, i: (b, h, 0, 0))],
        out_specs=[pl.BlockSpec((1, tq, w), lambda b, h, i: (b, i, h))],
        out_shape=[jax.ShapeDtypeStruct((nb, seq, heads * dh), F32)],
        args=[zq, zk, zv, ck],
        side_casts=side_casts or (),
        compiler_params=_params("parallel", "parallel", "parallel"),
        name="fox_attention",
    )
    return out if side_casts is None else (out, *side)


def _xattn_kernel(q_ref, k_ref, v_ref, o_ref, *, heads, dh):
    scores = [lax.dot_general(q_ref[:, hd * dh:(hd + 1) * dh], k_ref[:, hd * dh:(hd + 1) * dh],
                              (((1,), (1,)), ((), ())), preferred_element_type=F32)
              for hd in range(heads)]
    for hd in range(heads):
        sl = slice(hd * dh, (hd + 1) * dh)
        s = scores[hd]
        s = s - jnp.max(s, axis=-1, keepdims=True)
        p = jnp.exp(s)
        p = p / jnp.sum(p, axis=-1, keepdims=True)
        o_ref[:, sl] = jnp.dot(p.astype(BF16), v_ref[:, sl],
                               preferred_element_type=F32).astype(o_ref.dtype)


def cross_attention(q, k, v, *, nb, heads, tm=512):
    t, w = q.shape
    seq, mlen = t // nb, k.shape[0] // nb
    tm = min(tm, seq)
    nt = seq // tm
    return pl.pallas_call(
        functools.partial(_xattn_kernel, heads=heads, dh=w // heads),
        grid=(nb, nt),
        in_specs=[pl.BlockSpec((tm, w), lambda b, i: (b * nt + i, 0)),
                  pl.BlockSpec((mlen, w), lambda b, i: (b, 0)),
                  pl.BlockSpec((mlen, w), lambda b, i: (b, 0))],
        out_specs=pl.BlockSpec((tm, w), lambda b, i: (b * nt + i, 0)),
        out_shape=jax.ShapeDtypeStruct((t, w), BF16),
        compiler_params=_params("parallel", "parallel"),
        name="cross_attention",
    )(q, k, v)


def kernel(x, mem, ffn1_norm, ffn1_w_gate, ffn1_w_up, ffn1_w_down, mix_norm, w_in, ssm_A_re, ssm_A_im, ssm_log_step, ssm_B_re, ssm_B_im, ssm_C_re, ssm_C_im, ssm_D, ssm_w_glu, ssm_out_norm, fox_b_f, fox_q_norm, fox_k_norm, fox_out_norm, w_out, xattn_norm, mem_norm, xattn_wq, xattn_wk, xattn_wv, xattn_q_norm, xattn_k_norm, xattn_wo, ffn2_norm, ffn2_w_gate, ffn2_w_up, ffn2_w_down, final_norm):
    nb, seq, d = x.shape
    t = nb * seq
    depth = ffn1_norm.shape[0]
    g, n_state = ssm_A_re.shape[1:]
    p = ssm_B_re.shape[-1]
    w_ssm = g * p
    fox_h, fox_dh = fox_b_f.shape[-1], fox_q_norm.shape[-1]
    w_fox = fox_h * fox_dh
    o_q, o_k, o_v, o_f = w_ssm, w_ssm + w_fox, w_ssm + 2 * w_fox, w_ssm + 3 * w_fox
    x_dh = xattn_q_norm.shape[-1]
    x_w = xattn_wq.shape[-1]
    x_heads = x_w // x_dh
    mlen = mem.shape[1]
    tn_in = min(1024, w_ssm, w_fox)
    assert w_ssm % w_fox == 0 and w_ssm % tn_in == 0 and w_fox % tn_in == 0
    uv_lo_tiles, qk_tiles = w_ssm // tn_in, 2 * w_fox // tn_in

    h = x.reshape(t, d)
    mem2 = mem.reshape(nb * mlen, d)

    ffn1_w = (cast_bf16(ffn1_w_gate, 0), cast_bf16(ffn1_w_up, 0),
              cast_bf16(ffn1_w_down, 0, scale=0.5))
    w_in_b = cast_bf16(w_in, 0)

    for l in range(depth):
        more = l + 1 < depth
        h = ffn(h, ffn1_norm[l], *ffn1_w)

        hn = rmsnorm(h, mix_norm[l], BF16)
        gain_qk = jnp.concatenate([
            jnp.tile(fox_q_norm[l].astype(F32) * (fox_dh ** -0.5 * LOG2E), fox_h),
            jnp.tile(fox_k_norm[l].astype(F32), fox_h)])
        w_f = jnp.pad(w_in_b[:, o_f:], ((0, 0), (0, LANES - fox_h)))
        z_qk, w2_gate = matmul(
            hn, w_in_b, BF16, tm=1024, tn=tn_in, n_cols=2 * w_fox,
            w_col_tile=lambda j: j + uv_lo_tiles, norm_gain=gain_qk, norm_group=fox_dh,
            side_casts=[(ffn2_w_gate, l, 1.0)])
        z_uv, f_logit, w2_up = matmul(
            hn, w_in_b, BF16, tm=1024, tn=tn_in, n_cols=w_ssm + w_fox,
            w_col_tile=lambda j: jnp.where(j < uv_lo_tiles, j, j + qk_tiles), extra_w=w_f,
            side_casts=[(ffn2_w_up, l, 1.0)])
        z_qk3 = z_qk.reshape(nb, seq, 2 * w_fox)
        z3 = z_uv.reshape(nb, seq, w_ssm + w_fox)

        b_f_row = jnp.pad(fox_b_f[l].astype(F32), (0, LANES - fox_h)).reshape(1, LANES)
        c = forget_cumsum(f_logit.reshape(nb, seq, LANES), b_f_row)
        y_fox, wv_b, wo_b, w2_down, *next_rest = fox_attention(
            z_qk3, z_qk3, z3, c, q_off=0, k_off=w_fox, v_off=w_ssm, heads=fox_h, dh=fox_dh,
            side_casts=[(xattn_wv, l, 1.0), (xattn_wo, l, 1.0), (ffn2_w_down, l, 0.5)]
            + ([(ffn1_w_down, l + 1, 0.5), (w_in, l + 1, 1.0)] if more else []))
        ffn2_w = (w2_gate, w2_up, w2_down)

        abar_r, abar_i, bbar_r, bbar_i = s5_discretise(
            ssm_A_re[l], ssm_A_im[l], ssm_log_step[l], ssm_B_re[l], ssm_B_im[l])
        y_ssm, w_glu_b, w_out_b, wq_b, wk_b, *next_gate_up, y = s5_core(
            z3, abar_r, abar_i, bbar_r, bbar_i,
            ssm_C_re[l].astype(F32), ssm_C_im[l].astype(F32), ssm_D[l],
            side_casts=[(ssm_w_glu, l, 1.0), (w_out, l, 1.0), (xattn_wq, l, 1.0),
                        (xattn_wk, l, 1.0)]
            + ([(ffn1_w_gate, l + 1, 1.0), (ffn1_w_up, l + 1, 1.0)] if more else []),
            side_norm=(y_fox.reshape(t, w_fox), fox_out_norm[l], w_ssm + w_fox, w_ssm // w_fox))
        y = glu_norm(y_ssm.reshape(t, w_ssm), w_glu_b, ssm_out_norm[l], into=y)
        if more:
            ffn1_w = (*next_gate_up, next_rest[0])
            w_in_b = next_rest[1]

        h = matmul(y, w_out_b, F32, tm=1024, tn=1024, res=h)

        hn = rmsnorm(h, xattn_norm[l], BF16)
        mn = rmsnorm(mem2, mem_norm[l], BF16)
        q = matmul(hn, wq_b, BF16, tm=1024, tn=x_dh,
                   norm_gain=jnp.tile(xattn_q_norm[l].astype(F32) * (x_dh ** -0.5), x_heads),
                   norm_group=x_dh)
        k = matmul(mn, wk_b, BF16, tm=1024, tn=x_dh,
                   norm_gain=jnp.tile(xattn_k_norm[l].astype(F32), x_heads), norm_group=x_dh)
        v = matmul(mn, wv_b, BF16, tm=1024, tn=x_dh)
        o = cross_attention(q, k, v, nb=nb, heads=x_heads)
        h = matmul(o, wo_b, F32, tm=1024, tn=1024, res=h)

        h = ffn(h, ffn2_norm[l], *ffn2_w, final_gain=final_norm[l])

    return h.reshape(nb, seq, d)
```

```python
import functools
import math

import jax
import jax.numpy as jnp
from jax import lax
from jax.experimental import pallas as pl
from jax.experimental.pallas import tpu as pltpu

F32 = jnp.float32
BF16 = jnp.bfloat16
EPS = 1e-6
NEG_BIG = -1e30

V7X_VMEM_LIMIT_BYTES = 58 * 1024 * 1024
LANES = 128
SUBLANES = 8


def _params(*sem):
    return pltpu.CompilerParams(dimension_semantics=sem,
                                vmem_limit_bytes=V7X_VMEM_LIMIT_BYTES)


def _rms(x, g):
    ms = jnp.mean(x * x, axis=-1, keepdims=True)
    return x * lax.rsqrt(ms + EPS) * g


def _cast_kernel(w_ref, o_ref, *, scale):
    w = w_ref[...]
    if scale != 1.0:
        w = w * scale
    o_ref[...] = w.astype(o_ref.dtype)


def cast_bf16(w_stack, layer, *, scale=1.0, tr=256):
    _, k, n = w_stack.shape
    tr = min(tr, k)
    assert k % tr == 0
    return pl.pallas_call(
        functools.partial(_cast_kernel, scale=scale),
        grid=(k // tr,),
        in_specs=[pl.BlockSpec((None, tr, n), lambda i: (layer, i, 0))],
        out_specs=pl.BlockSpec((tr, n), lambda i: (i, 0)),
        out_shape=jax.ShapeDtypeStruct((k, n), BF16),
        compiler_params=_params("parallel"),
        name="cast_bf16",
    )(w_stack)

BF16_SUBLANE_TILE = 16


def _side_block(k, n, steps):
    if k % steps == 0 and (k // steps) % BF16_SUBLANE_TILE == 0:
        return (k // steps, n), lambda s: (s, 0)
    for s1 in (16, 8, 4, 2, 1):
        s2 = steps // s1
        if (steps % s1 == 0 and k % s1 == 0 and (k // s1) % BF16_SUBLANE_TILE == 0
                and n % s2 == 0 and (n // s2) % LANES == 0):
            return (k // s1, n // s2), lambda s: (s // s2, s % s2)
    raise ValueError(f"cannot tile a ({k}, {n}) weight in {steps} blocks")


def _pallas_with_side_casts(body, *, grid, in_specs, out_specs, out_shape, args, side_casts=(),
                            side_norm=None, **kwargs):
    out_specs, out_shape = list(out_specs), list(out_shape)
    n_in, n_out, n_side = len(in_specs), len(out_specs), len(side_casts)
    steps = math.prod(grid)
    strides = [math.prod(grid[a + 1:]) for a in range(len(grid))]

    def linear_step(*g):
        return sum(gi * st for gi, st in zip(g, strides))

    if side_norm is not None:
        nx, ngain, out_cols, col_block = side_norm
        nm, nk = nx.shape
        assert nm % steps == 0 and (nm // steps) % BF16_SUBLANE_TILE == 0
        norm_in_specs = [pl.BlockSpec((nm // steps, nk), lambda *g: (linear_step(*g), 0)),
                         pl.BlockSpec((1, nk), lambda *g: (0, 0))]
        norm_out_spec = pl.BlockSpec((nm // steps, nk), lambda *g: (linear_step(*g), col_block))
        norm_out_shape = jax.ShapeDtypeStruct((nm, out_cols), BF16)
        norm_args = [nx, ngain.reshape(1, nk).astype(F32)]
    scales = []
    for w_stack, layer, scale in side_casts:
        _, k, n = w_stack.shape
        blk, to_index = _side_block(k, n, steps)

        def index(*g, _to_index=to_index):
            return _to_index(linear_step(*g))

        in_specs = in_specs + [pl.BlockSpec((None,) + blk,
                                            lambda *g, _l=layer, _ix=index: (_l,) + _ix(*g))]
        out_specs.append(pl.BlockSpec(blk, index))
        out_shape.append(jax.ShapeDtypeStruct((k, n), BF16))
        args = list(args) + [w_stack]
        scales.append(scale)

    n_norm_in = 0
    if side_norm is not None:
        n_norm_in = len(norm_in_specs)
        in_specs = in_specs + norm_in_specs
        out_specs.append(norm_out_spec)
        out_shape.append(norm_out_shape)
        args = list(args) + norm_args
    n_all_in = n_in + n_side + n_norm_in
    n_side_out = n_side + (1 if side_norm is not None else 0)

    def kern(*refs):
        side_in = refs[n_in:n_in + n_side]
        outs = refs[n_all_in:n_all_in + n_out]
        side_out = refs[n_all_in + n_out:n_all_in + n_out + n_side_out]
        scratch = refs[n_all_in + n_out + n_side_out:]
        for si, so, scale in zip(side_in, side_out, scales):
            w = si[...]
            so[...] = (w if scale == 1.0 else w * scale).astype(so.dtype)
        if side_norm is not None:
            nx_ref, ng_ref = refs[n_in + n_side:n_all_in]
            side_out[-1][...] = _rms(nx_ref[...], ng_ref[...]).astype(BF16)
        body(*refs[:n_in], *outs, *scratch)

    res = pl.pallas_call(kern, grid=grid, in_specs=in_specs, out_specs=out_specs,
                         out_shape=out_shape, **kwargs)(*args)
    return tuple(res[:n_out]), tuple(res[n_out:])


def _rmsnorm_kernel(x_ref, g_ref, o_ref):
    o_ref[...] = _rms(x_ref[...].astype(F32), g_ref[...]).astype(o_ref.dtype)


def rmsnorm(x, g, out_dtype, tm=512):
    m, k = x.shape
    tm = min(tm, m)
    return pl.pallas_call(
        _rmsnorm_kernel,
        grid=(m // tm,),
        in_specs=[pl.BlockSpec((tm, k), lambda i: (i, 0)),
                  pl.BlockSpec((1, k), lambda i: (0, 0))],
        out_specs=pl.BlockSpec((tm, k), lambda i: (i, 0)),
        out_shape=jax.ShapeDtypeStruct((m, k), out_dtype),
        compiler_params=_params("parallel"),
        name="rmsnorm",
    )(x, g.reshape(1, k).astype(F32))


NORM_ROW_CHUNKS = 4


def _mm_kernel(*refs, has_res, norm_group, has_extra):
    x_ref, w_ref = refs[0], refs[1]
    pos = 2
    res_ref = g_ref = None
    if has_res:
        res_ref = refs[pos]
        pos += 1
    if norm_group:
        g_ref = refs[pos]
        pos += 1
    if has_extra:
        xw_ref, o_ref, xo_ref = refs[pos], refs[pos + 1], refs[pos + 2]

        @pl.when(pl.program_id(1) == 0)
        def _():
            xo_ref[...] = jnp.dot(x_ref[...], xw_ref[...], preferred_element_type=F32)
    else:
        o_ref = refs[pos]

    if not norm_group:
        acc = jnp.dot(x_ref[...], w_ref[...], preferred_element_type=F32)
        if has_res:
            acc = acc + res_ref[...]
        o_ref[...] = acc.astype(o_ref.dtype)
        return

    tm = x_ref.shape[0]
    chunks = NORM_ROW_CHUNKS if tm % (NORM_ROW_CHUNKS * BF16_SUBLANE_TILE) == 0 else 1
    ch = tm // chunks
    accs = [jnp.dot(x_ref[r * ch:(r + 1) * ch, :], w_ref[...], preferred_element_type=F32)
            for r in range(chunks)]
    for r, acc in enumerate(accs):
        rows = slice(r * ch, (r + 1) * ch)
        if has_res:
            acc = acc + res_ref[rows, :]
        for c in range(acc.shape[1] // norm_group):
            sl = slice(c * norm_group, (c + 1) * norm_group)
            o_ref[rows, sl] = _rms(acc[:, sl], g_ref[:, sl]).astype(o_ref.dtype)


def matmul(x, w, out_dtype, *, tm=512, tn=512, n_cols=None, w_col_tile=None, res=None,
           norm_gain=None, norm_group=0, side_casts=None, extra_w=None):
    m, k = x.shape
    n = w.shape[1] if n_cols is None else n_cols
    tm, tn = min(tm, m), min(tn, n)
    assert m % tm == 0 and n % tn == 0
    if w_col_tile is None:
        w_col_tile = lambda j: j
    in_specs = [pl.BlockSpec((tm, k), lambda i, j: (i, 0)),
                pl.BlockSpec((k, tn), lambda i, j: (0, w_col_tile(j)))]
    args = [x, w]
    if res is not None:
        in_specs.append(pl.BlockSpec((tm, tn), lambda i, j: (i, j)))
        args.append(res)
    if norm_group:
        assert tn % norm_group == 0
        in_specs.append(pl.BlockSpec((1, tn), lambda i, j: (0, j)))
        args.append(norm_gain.reshape(1, n).astype(F32))
    out_specs = [pl.BlockSpec((tm, tn), lambda i, j: (i, j))]
    out_shape = [jax.ShapeDtypeStruct((m, n), out_dtype)]
    if extra_w is not None:
        assert side_casts is not None
        e = extra_w.shape[1]
        in_specs.append(pl.BlockSpec((k, e), lambda i, j: (0, 0)))
        args.append(extra_w)
        out_specs.append(pl.BlockSpec((tm, e), lambda i, j: (i, 0)))
        out_shape.append(jax.ShapeDtypeStruct((m, e), F32))
    outs, side = _pallas_with_side_casts(
        functools.partial(_mm_kernel, has_res=res is not None, norm_group=norm_group,
                          has_extra=extra_w is not None),
        grid=(m // tm, n // tn),
        in_specs=in_specs,
        out_specs=out_specs,
        out_shape=out_shape,
        args=args,
        side_casts=side_casts or (),
        compiler_params=_params("parallel", "arbitrary" if extra_w is not None else "parallel"),
        name="matmul",
    )
    return outs[0] if side_casts is None else (*outs, *side)


def _ffn_kernel(*refs, final_norm):
    if final_norm:
        x_ref, g_ref, wg_ref, wu_ref, wd_ref, fg_ref, o_ref, n_scr = refs
    else:
        x_ref, g_ref, wg_ref, wu_ref, wd_ref, o_ref, n_scr = refs
        fg_ref = None
    j = pl.program_id(1)

    @pl.when(j == 0)
    def _():
        x = x_ref[...]
        n_scr[...] = _rms(x, g_ref[...]).astype(BF16)
        o_ref[...] = x

    n = n_scr[...]
    gate = jnp.dot(n, wg_ref[...], preferred_element_type=F32)
    up = jnp.dot(n, wu_ref[...], preferred_element_type=F32)
    hid = (gate * jax.nn.sigmoid(gate) * up).astype(BF16)
    o_ref[...] += jnp.dot(hid, wd_ref[...], preferred_element_type=F32)

    if final_norm:
        @pl.when(j == pl.num_programs(1) - 1)
        def _():
            o_ref[...] = _rms(o_ref[...], fg_ref[...])


def ffn(x, g, wg, wu, wd_half, final_gain=None, *, tm=512, tf=256):
    m, d = x.shape
    dff = wg.shape[1]
    tm, tf = min(tm, m), min(tf, dff)
    assert m % tm == 0 and dff % tf == 0
    in_specs = [pl.BlockSpec((tm, d), lambda i, j: (i, 0)),
                pl.BlockSpec((1, d), lambda i, j: (0, 0)),
                pl.BlockSpec((d, tf), lambda i, j: (0, j)),
                pl.BlockSpec((d, tf), lambda i, j: (0, j)),
                pl.BlockSpec((tf, d), lambda i, j: (j, 0))]
    args = [x, g.reshape(1, d).astype(F32), wg, wu, wd_half]
    if final_gain is not None:
        in_specs.append(pl.BlockSpec((1, d), lambda i, j: (0, 0)))
        args.append(final_gain.reshape(1, d).astype(F32))
    return pl.pallas_call(
        functools.partial(_ffn_kernel, final_norm=final_gain is not None),
        grid=(m // tm, dff // tf),
        in_specs=in_specs,
        out_specs=pl.BlockSpec((tm, d), lambda i, j: (i, 0)),
        out_shape=jax.ShapeDtypeStruct((m, d), F32),
        scratch_shapes=[pltpu.VMEM((tm, d), BF16)],
        compiler_params=_params("parallel", "arbitrary"),
        name="ffn",
    )(*args)


def _s5_disc_kernel(are_ref, aim_ref, ls_ref, bre_ref, bim_ref,
                    abr_ref, abi_ref, bbr_ref, bbi_ref):
    ar, ai = are_ref[...], aim_ref[...]
    dt = jnp.exp(ls_ref[...])
    mag = jnp.exp(ar * dt)
    abar_r = mag * jnp.cos(ai * dt)
    abar_i = mag * jnp.sin(ai * dt)
    den = ar * ar + ai * ai
    pr, pi_ = abar_r - 1.0, abar_i
    coef_r = (pr * ar + pi_ * ai) / den
    coef_i = (pi_ * ar - pr * ai) / den
    br, bi = bre_ref[...], bim_ref[...]
    abr_ref[...] = abar_r
    abi_ref[...] = abar_i
    bbr_ref[...] = coef_r * br - coef_i * bi
    bbi_ref[...] = coef_r * bi + coef_i * br


def s5_discretise(a_re, a_im, log_step, b_re, b_im):
    g, n = a_re.shape
    p = b_re.shape[-1]
    col = jax.ShapeDtypeStruct((g, n, 1), F32)
    mat = jax.ShapeDtypeStruct((g, n, p), F32)
    ls = jnp.broadcast_to(log_step.reshape(g, 1, 1), (g, n, 1))
    abr, abi, bbr, bbi = pl.pallas_call(
        _s5_disc_kernel,
        out_shape=(col, col, mat, mat),
        compiler_params=pltpu.CompilerParams(vmem_limit_bytes=V7X_VMEM_LIMIT_BYTES),
        name="s5_discretise",
    )(a_re.reshape(g, n, 1), a_im.reshape(g, n, 1), ls, b_re, b_im)
    return abr.reshape(g, n), abi.reshape(g, n), bbr, bbi


GROUPS_PER_HALF = 8


def _gelu_tanh(x):
    c = math.sqrt(2.0 / math.pi)
    return 0.5 * x * (1.0 + jnp.tanh(c * (x + 0.044715 * (x * x * x))))


def _s5_kernel(u_ref, rb_ref, rc_ref, ar_ref, ai_ref, d_ref, y_ref,
               lhs_scr, sr_scr, si_scr, o2_scr, xr_scr, xi_scr, *, nb, hs, tt, sw, hw):
    @pl.when(pl.program_id(1) == 0)
    def _():
        xr_scr[...] = jnp.zeros_like(xr_scr)
        xi_scr[...] = jnp.zeros_like(xi_scr)

    zeros = jnp.zeros((tt, hw), F32)
    for b in range(nb):
        ub = u_ref[b].astype(F32)
        for hh in range(hs):
            rows = pl.ds(hh * nb + b, tt, stride=SUBLANES)
            for c in range(hs):
                lhs_scr[c, rows, :] = ub[:, c * hw:(c + 1) * hw] if c == hh else zeros

    lhs = jnp.concatenate([lhs_scr[c] for c in range(hs)], axis=-1).astype(BF16)
    hm = lhs.shape[0] // 2
    for r0 in (0, hm):
        sr_scr[r0:r0 + hm, :] = jnp.dot(lhs[r0:r0 + hm], rb_ref[0, :, :sw],
                                        preferred_element_type=F32)
        si_scr[r0:r0 + hm, :] = jnp.dot(lhs[r0:r0 + hm], rb_ref[0, :, sw:],
                                        preferred_element_type=F32)

    ar, ai = ar_ref[0], ai_ref[0]

    def step(t, carry):
        xr, xi = carry
        rows = pl.ds(pl.multiple_of(t * SUBLANES, SUBLANES), SUBLANES)
        nxr = ar * xr - ai * xi + sr_scr[rows, :]
        nxi = ar * xi + ai * xr + si_scr[rows, :]
        sr_scr[rows, :] = nxr
        si_scr[rows, :] = nxi
        return nxr, nxi

    xr, xi = lax.fori_loop(0, tt, step, (xr_scr[...], xi_scr[...]), unroll=8)
    xr_scr[...] = xr
    xi_scr[...] = xi

    for r0 in (0, hm):
        o2 = (jnp.dot(sr_scr[r0:r0 + hm, :].astype(BF16), rc_ref[0, :sw, :],
                      preferred_element_type=F32)
              + jnp.dot(si_scr[r0:r0 + hm, :].astype(BF16), rc_ref[0, sw:, :],
                        preferred_element_type=F32))
        for c in range(hs):
            o2_scr[c, r0:r0 + hm, :] = o2[:, c * hw:(c + 1) * hw]

    for b in range(nb):
        parts = [o2_scr[hh, pl.ds(hh * nb + b, tt, stride=SUBLANES), :] for hh in range(hs)]
        yb = jnp.concatenate(parts, axis=-1) + d_ref[...] * u_ref[b].astype(F32)
        y_ref[b] = _gelu_tanh(yb).astype(y_ref.dtype)


def s5_core(z3, abar_r, abar_i, bbar_r, bbar_i, c_re, c_im, d_skip, *, tt=256, side_casts=None,
            side_norm=None):
    nb, seq, _ = z3.shape
    g, n, p = bbar_r.shape
    assert SUBLANES % nb == 0
    hs = SUBLANES // nb
    gph = GROUPS_PER_HALF
    assert g % (hs * gph) == 0
    nblk = g // (hs * gph)
    hw, sw = gph * p, gph * n
    cw = hs * hw
    assert hw == LANES and sw % LANES == 0
    tt = min(tt, seq)
    assert seq % tt == 0

    eye = jnp.eye(gph, dtype=F32)

    def in_proj(bb):
        bb = bb.reshape(nblk, hs, gph, n, p)
        return jnp.einsum("jhqnp,qr->jhqprn", bb, eye).reshape(nblk, cw, sw)

    def out_proj(cc):
        cc = cc.reshape(nblk, hs, gph, p, n)
        return jnp.einsum("jhrpn,qr->jqnhrp", cc, eye).reshape(nblk, sw, cw)

    rb = jnp.concatenate([in_proj(bbar_r), in_proj(bbar_i)], axis=-1).astype(BF16)
    rc = jnp.concatenate([out_proj(c_re), out_proj(-c_im)], axis=1).astype(BF16)

    def rows(a):
        a = a.reshape(nblk, hs, 1, sw)
        return jnp.broadcast_to(a, (nblk, hs, nb, sw)).reshape(nblk, hs * nb, sw)

    (y,), side = _pallas_with_side_casts(
        functools.partial(_s5_kernel, nb=nb, hs=hs, tt=tt, sw=sw, hw=hw),
        grid=(nblk, seq // tt),
        in_specs=[pl.BlockSpec((nb, tt, cw), lambda j, t: (0, t, j)),
                  pl.BlockSpec((1, cw, 2 * sw), lambda j, t: (j, 0, 0)),
                  pl.BlockSpec((1, 2 * sw, cw), lambda j, t: (j, 0, 0)),
                  pl.BlockSpec((1, SUBLANES, sw), lambda j, t: (j, 0, 0)),
                  pl.BlockSpec((1, SUBLANES, sw), lambda j, t: (j, 0, 0)),
                  pl.BlockSpec((1, cw), lambda j, t: (0, j))],
        out_specs=[pl.BlockSpec((nb, tt, cw), lambda j, t: (0, t, j))],
        out_shape=[jax.ShapeDtypeStruct((nb, seq, g * p), BF16)],
        args=[z3, rb, rc, rows(abar_r), rows(abar_i), d_skip.reshape(1, g * p).astype(F32)],
        side_casts=side_casts or (),
        side_norm=side_norm,
        scratch_shapes=[pltpu.VMEM((hs, SUBLANES * tt, hw), F32),
                        pltpu.VMEM((SUBLANES * tt, sw), F32),
                        pltpu.VMEM((SUBLANES * tt, sw), F32),
                        pltpu.VMEM((hs, SUBLANES * tt, hw), F32),
                        pltpu.VMEM((SUBLANES, sw), F32),
                        pltpu.VMEM((SUBLANES, sw), F32)],
        compiler_params=_params("parallel", "arbitrary"),
        name="s5_core",
    )
    return y if side_casts is None and side_norm is None else (y, *side)


def _glu_kernel(y_ref, w_ref, g_ref, into_ref, o_ref):
    del into_ref
    tm = y_ref.shape[0]
    chunks = NORM_ROW_CHUNKS if tm % (NORM_ROW_CHUNKS * BF16_SUBLANE_TILE) == 0 else 1
    ch = tm // chunks
    vs = [jnp.dot(y_ref[r * ch:(r + 1) * ch, :], w_ref[...], preferred_element_type=F32)
          for r in range(chunks)]
    for r, v in enumerate(vs):
        rows = slice(r * ch, (r + 1) * ch)
        o = y_ref[rows, :].astype(F32) * jax.nn.sigmoid(v)
        o_ref[rows, :] = _rms(o, g_ref[...]).astype(o_ref.dtype)


def glu_norm(y, w, g, *, into, tm=512):
    m, k = y.shape
    tm = min(tm, m)
    assert into.shape[0] == m and into.shape[1] % k == 0 and into.dtype == BF16
    return pl.pallas_call(
        _glu_kernel,
        grid=(m // tm,),
        in_specs=[pl.BlockSpec((tm, k), lambda i: (i, 0)),
                  pl.BlockSpec((k, k), lambda i: (0, 0)),
                  pl.BlockSpec((1, k), lambda i: (0, 0)),
                  pl.BlockSpec(memory_space=pl.ANY)],
        out_specs=pl.BlockSpec((tm, k), lambda i: (i, 0)),
        out_shape=jax.ShapeDtypeStruct(into.shape, BF16),
        input_output_aliases={3: 0},
        compiler_params=_params("parallel"),
        name="glu_norm",
    )(y, w, g.reshape(1, k).astype(F32), into)


LOG2E = math.log2(math.e)
def _cumsum_kernel(f_ref, bf_ref, c_ref, *, chunk):
    seq = f_ref.shape[1]
    row = lax.broadcasted_iota(jnp.int32, (chunk, chunk), 0)
    col = lax.broadcasted_iota(jnp.int32, (chunk, chunk), 1)
    tri = (col <= row).astype(F32)
    carry = jnp.zeros((1, f_ref.shape[2]), F32)
    for ci in range(seq // chunk):
        sl = slice(ci * chunk, (ci + 1) * chunk)
        x = f_ref[0, sl, :] + bf_ref[...]
        log_f = (jnp.minimum(x, 0.0) - jnp.log1p(jnp.exp(-jnp.abs(x)))) * LOG2E
        cs = jnp.dot(tri, log_f, precision=lax.Precision.HIGHEST,
                     preferred_element_type=F32) + carry
        c_ref[0, sl, :] = cs
        carry = cs[chunk - 1:chunk, :]


def forget_cumsum(f3, b_f_row, *, chunk=256):
    nb, seq, w = f3.shape
    chunk = min(chunk, seq)
    return pl.pallas_call(
        functools.partial(_cumsum_kernel, chunk=chunk),
        grid=(nb,),
        in_specs=[pl.BlockSpec((1, seq, w), lambda b: (b, 0, 0)),
                  pl.BlockSpec((1, w), lambda b: (0, 0))],
        out_specs=pl.BlockSpec((1, seq, w), lambda b: (b, 0, 0)),
        out_shape=jax.ShapeDtypeStruct((nb, seq, w), F32),
        compiler_params=_params("parallel"),
        name="forget_cumsum",
    )(f3, b_f_row)


FOX_HEADS_PER_STEP = 2


def _fox_kernel(q_ref, k_ref, v_ref, ck_ref, o_ref, *, tq, big, nh, dh):
    i = pl.program_id(2)
    per_big = big // tq

    q_rows = pl.ds(pl.multiple_of(i * tq, tq), tq)
    cq_cols = [jnp.transpose(jnp.broadcast_to(ck_ref[0, hd, :, q_rows], (LANES, tq)))[:, :1]
               for hd in range(nh)]

    def scores(hd, start, width, diagonal):
        cols = slice(hd * dh, (hd + 1) * dh)
        rows = pl.ds(start, width)
        s = lax.dot_general(q_ref[0, :, cols], k_ref[0, rows, cols], (((1,), (1,)), ((), ())),
                            preferred_element_type=F32)
        s = s + (cq_cols[hd] - ck_ref[0, hd, :, rows])
        if diagonal:
            r = lax.broadcasted_iota(jnp.int32, s.shape, 0)
            c = lax.broadcasted_iota(jnp.int32, s.shape, 1)
            s = jnp.where(c <= r, s, NEG_BIG)
        return s

    def update(hd, s, start, width, carry):
        m, l, acc = carry
        v = v_ref[0, pl.ds(start, width), hd * dh:(hd + 1) * dh]
        m_new = jnp.maximum(m, jnp.max(s, axis=-1, keepdims=True))
        alpha = jnp.exp2(m - m_new)
        p = jnp.exp2(s - m_new)
        l = alpha * l + jnp.sum(p, axis=-1, keepdims=True)
        acc = alpha * acc + jnp.dot(p.astype(v.dtype), v, preferred_element_type=F32)
        return m_new, l, acc

    def block(start, width, carries, diagonal):
        ss = [scores(hd, start, width, diagonal) for hd in range(nh)]
        return tuple(update(hd, ss[hd], start, width, carries[hd]) for hd in range(nh))

    init = tuple((jnp.full((tq, 1), NEG_BIG, F32), jnp.zeros((tq, 1), F32),
                  jnp.zeros((tq, dh), F32)) for _ in range(nh))
    n_big = i // per_big
    carries = lax.fori_loop(
        0, n_big, lambda j, c: block(pl.multiple_of(j * big, big), big, c, False), init)
    carries = lax.fori_loop(
        n_big * per_big, i, lambda j, c: block(pl.multiple_of(j * tq, tq), tq, c, False), carries)
    carries = block(pl.multiple_of(i * tq, tq), tq, carries, True)
    for hd in range(nh):
        _, l, acc = carries[hd]
        o_ref[0, :, hd * dh:(hd + 1) * dh] = (acc / l).astype(o_ref.dtype)


def fox_attention(zq, zk, zv, c, *, q_off, k_off, v_off, heads, dh, tq=1024, big=1024,
                  side_casts=None):
    nb, seq, _ = zq.shape
    tq = min(tq, seq)
    big = min(big, seq)
    nq = seq // tq
    nh = FOX_HEADS_PER_STEP if heads % FOX_HEADS_PER_STEP == 0 else 1
    w = nh * dh
    assert seq % tq == 0 and big % tq == 0
    assert q_off % w == 0 and k_off % w == 0 and v_off % w == 0
    ch = jnp.transpose(c[:, :, :heads], (0, 2, 1))
    ck = ch.reshape(nb, heads, 1, seq)
    qb, kb, vb = q_off // w, k_off // w, v_off // w
    (out,), side = _pallas_with_side_casts(
        functools.partial(_fox_kernel, tq=tq, big=big, nh=nh, dh=dh),
        grid=(nb, heads // nh, nq),
        in_specs=[pl.BlockSpec((1, tq, w), lambda b, h, i: (b, i, qb + h)),
                  pl.BlockSpec((1, seq, w), lambda b, h, i: (b, 0, kb + h)),
                  pl.BlockSpec((1, seq, w), lambda b, h, i: (b, 0, vb + h)),
                  pl.BlockSpec((1, nh, 1, seq), lambda b, h, i: (b, h, 0, 0))],
        out_specs=[pl.BlockSpec((1, tq, w), lambda b, h, i: (b, i, h))],
        out_shape=[jax.ShapeDtypeStruct((nb, seq, heads * dh), F32)],
        args=[zq, zk, zv, ck],
        side_casts=side_casts or (),
        compiler_params=_params("parallel", "parallel", "parallel"),
        name="fox_attention",
    )
    return out if side_casts is None else (out, *side)


def _xattn_kernel(q_ref, k_ref, v_ref, o_ref, *, heads, dh):
    scores = [lax.dot_general(q_ref[:, hd * dh:(hd + 1) * dh], k_ref[:, hd * dh:(hd + 1) * dh],
                              (((1,), (1,)), ((), ())), preferred_element_type=F32)
              for hd in range(heads)]
    for hd in range(heads):
        sl = slice(hd * dh, (hd + 1) * dh)
        s = scores[hd]
        s = s - jnp.max(s, axis=-1, keepdims=True)
        p = jnp.exp(s)
        p = p / jnp.sum(p, axis=-1, keepdims=True)
        o_ref[:, sl] = jnp.dot(p.astype(BF16), v_ref[:, sl],
                               preferred_element_type=F32).astype(o_ref.dtype)


def cross_attention(q, k, v, *, nb, heads, tm=1024):
    t, w = q.shape
    seq, mlen = t // nb, k.shape[0] // nb
    tm = min(tm, seq)
    nt = seq // tm
    return pl.pallas_call(
        functools.partial(_xattn_kernel, heads=heads, dh=w // heads),
        grid=(nb, nt),
        in_specs=[pl.BlockSpec((tm, w), lambda b, i: (b * nt + i, 0)),
                  pl.BlockSpec((mlen, w), lambda b, i: (b, 0)),
                  pl.BlockSpec((mlen, w), lambda b, i: (b, 0))],
        out_specs=pl.BlockSpec((tm, w), lambda b, i: (b * nt + i, 0)),
        out_shape=jax.ShapeDtypeStruct((t, w), BF16),
        compiler_params=_params("parallel", "parallel"),
        name="cross_attention",
    )(q, k, v)


def kernel(x, mem, ffn1_norm, ffn1_w_gate, ffn1_w_up, ffn1_w_down, mix_norm, w_in, ssm_A_re, ssm_A_im, ssm_log_step, ssm_B_re, ssm_B_im, ssm_C_re, ssm_C_im, ssm_D, ssm_w_glu, ssm_out_norm, fox_b_f, fox_q_norm, fox_k_norm, fox_out_norm, w_out, xattn_norm, mem_norm, xattn_wq, xattn_wk, xattn_wv, xattn_q_norm, xattn_k_norm, xattn_wo, ffn2_norm, ffn2_w_gate, ffn2_w_up, ffn2_w_down, final_norm):
    nb, seq, d = x.shape
    t = nb * seq
    depth = ffn1_norm.shape[0]
    g, n_state = ssm_A_re.shape[1:]
    p = ssm_B_re.shape[-1]
    w_ssm = g * p
    fox_h, fox_dh = fox_b_f.shape[-1], fox_q_norm.shape[-1]
    w_fox = fox_h * fox_dh
    o_f = w_ssm + 3 * w_fox
    x_dh = xattn_q_norm.shape[-1]
    x_w = xattn_wq.shape[-1]
    x_heads = x_w // x_dh
    mlen = mem.shape[1]
    tn_in = min(1024, w_ssm, w_fox)
    assert w_ssm % w_fox == 0 and w_ssm % tn_in == 0 and w_fox % tn_in == 0
    uv_lo_tiles, qk_tiles = w_ssm // tn_in, 2 * w_fox // tn_in

    h = x.reshape(t, d)
    mem2 = mem.reshape(nb * mlen, d)

    ffn1_w = (cast_bf16(ffn1_w_gate, 0), cast_bf16(ffn1_w_up, 0),
              cast_bf16(ffn1_w_down, 0, scale=0.5))
    w_in_b = cast_bf16(w_in, 0)

    for l in range(depth):
        more = l + 1 < depth
        h = ffn(h, ffn1_norm[l], *ffn1_w)

        hn = rmsnorm(h, mix_norm[l], BF16)
        gain_qk = jnp.concatenate([
            jnp.tile(fox_q_norm[l].astype(F32) * (fox_dh ** -0.5 * LOG2E), fox_h),
            jnp.tile(fox_k_norm[l].astype(F32), fox_h)])
        w_f = jnp.pad(w_in_b[:, o_f:], ((0, 0), (0, LANES - fox_h)))
        z_qk, w2_gate = matmul(
            hn, w_in_b, BF16, tm=1024, tn=tn_in, n_cols=2 * w_fox,
            w_col_tile=lambda j: j + uv_lo_tiles, norm_gain=gain_qk, norm_group=fox_dh,
            side_casts=[(ffn2_w_gate, l, 1.0)])
        z_uv, f_logit, w2_up = matmul(
            hn, w_in_b, BF16, tm=1024, tn=tn_in, n_cols=w_ssm + w_fox,
            w_col_tile=lambda j: jnp.where(j < uv_lo_tiles, j, j + qk_tiles), extra_w=w_f,
            side_casts=[(ffn2_w_up, l, 1.0)])
        z_qk3 = z_qk.reshape(nb, seq, 2 * w_fox)
        z3 = z_uv.reshape(nb, seq, w_ssm + w_fox)

        b_f_row = jnp.pad(fox_b_f[l].astype(F32), (0, LANES - fox_h)).reshape(1, LANES)
        c = forget_cumsum(f_logit.reshape(nb, seq, LANES), b_f_row)
        y_fox, wv_b, wo_b, w2_down, *next_rest = fox_attention(
            z_qk3, z_qk3, z3, c, q_off=0, k_off=w_fox, v_off=w_ssm, heads=fox_h, dh=fox_dh,
            side_casts=[(xattn_wv, l, 1.0), (xattn_wo, l, 1.0), (ffn2_w_down, l, 0.5)]
            + ([(ffn1_w_down, l + 1, 0.5), (w_in, l + 1, 1.0)] if more else []))
        ffn2_w = (w2_gate, w2_up, w2_down)

        abar_r, abar_i, bbar_r, bbar_i = s5_discretise(
            ssm_A_re[l], ssm_A_im[l], ssm_log_step[l], ssm_B_re[l], ssm_B_im[l])
        y_ssm, w_glu_b, w_out_b, wq_b, wk_b, *next_gate_up, y = s5_core(
            z3, abar_r, abar_i, bbar_r, bbar_i,
            ssm_C_re[l].astype(F32), ssm_C_im[l].astype(F32), ssm_D[l],
            side_casts=[(ssm_w_glu, l, 1.0), (w_out, l, 1.0), (xattn_wq, l, 1.0),
                        (xattn_wk, l, 1.0)]
            + ([(ffn1_w_gate, l + 1, 1.0), (ffn1_w_up, l + 1, 1.0)] if more else []),
            side_norm=(y_fox.reshape(t, w_fox), fox_out_norm[l], w_ssm + w_fox, w_ssm // w_fox))
        y = glu_norm(y_ssm.reshape(t, w_ssm), w_glu_b, ssm_out_norm[l], into=y)
        if more:
            ffn1_w = (*next_gate_up, next_rest[0])
            w_in_b = next_rest[1]

        h = matmul(y, w_out_b, F32, tm=1024, tn=1024, res=h)

        hn = rmsnorm(h, xattn_norm[l], BF16)
        mn = rmsnorm(mem2, mem_norm[l], BF16)
        q = matmul(hn, wq_b, BF16, tm=1024, tn=x_dh,
                   norm_gain=jnp.tile(xattn_q_norm[l].astype(F32) * (x_dh ** -0.5), x_heads),
                   norm_group=x_dh)
        k = matmul(mn, wk_b, BF16, tm=1024, tn=x_dh,
                   norm_gain=jnp.tile(xattn_k_norm[l].astype(F32), x_heads), norm_group=x_dh)
        v = matmul(mn, wv_b, BF16, tm=1024, tn=x_dh)
        o = cross_attention(q, k, v, nb=nb, heads=x_heads)
        h = matmul(o, wo_b, F32, tm=1024, tn=1024, res=h)

        h = ffn(h, ffn2_norm[l], *ffn2_w, final_gain=final_norm[l])

    return h.reshape(nb, seq, d)
```
